```python
import jax, jax.numpy as jnp
from jax import lax
import numpy as np


D_MODEL = 2048
BATCH = 4
SEQ = 4096
DEPTH = 2

GRID_W = 64
CTX_LEN = 256
N_MIXERS = 2
N_HGRN_LAYERS = (DEPTH + 1) // 2
N_LRU_LAYERS = DEPTH // 2

HG_HEADS = 16
HG_DK = 128
HG_DV = D_MODEL // HG_HEADS
HG_FDIM = HG_HEADS * HG_DK
HG_VDIM = HG_HEADS * HG_DV
HG_CHUNK = 64

LRU_WIDTH = D_MODEL
LRU_HEADS = 8
LRU_BLOCK = LRU_WIDTH // LRU_HEADS
CONV_W = 4
CONV_PAD_LEFT = 2
LRU_C = 8.0

N_EXPERTS = 32
TOP_K = 4
D_EXPERT = D_MODEL
SWIGLU_LIMIT = 7.0
SWIGLU_ALPHA = 1.702
MOE_BLOCK = 128

DN_ALPHA = float((2 * DEPTH) ** 0.25)
DN_BETA = float((8 * DEPTH) ** -0.25)
LN_EPS = 1e-5
RMS_EPS = 1e-6

kernel_name = 'hybrid_hgrn2_rglru_moe_prefix_dit'


def layer_norm(x, g, b):
    xf = x.astype(jnp.float32)
    mu = jnp.mean(xf, -1, keepdims=True)
    var = jnp.mean(jnp.square(xf - mu), -1, keepdims=True)
    return ((xf - mu) * lax.rsqrt(var + LN_EPS) * g + b).astype(x.dtype)


def modulate(h, shift, scale):
    return h * (1 + scale) + shift


def _heads(a, n_heads):
    return a.reshape(a.shape[0], a.shape[1], n_heads, -1)


def hgrn2_chunk_scan(q, k, v, logf, s0):
    bsz, t, h, _ = q.shape
    dv = v.shape[-1]
    nc = t // HG_CHUNK

    def to_chunks(a):
        return jnp.moveaxis(a.reshape(bsz, nc, HG_CHUNK, h, a.shape[-1]), 1, 0)

    mask = jnp.tril(jnp.ones((HG_CHUNK, HG_CHUNK), bool))[None, :, :, None, None]

    def step(s, inp):
        qi, ki, vi, lfi = inp
        cum = jnp.cumsum(lfi.astype(jnp.float32), axis=1)
        o_inter = jnp.einsum('bchk,bhkv->bchv', qi * jnp.exp(cum), s)
        diff = cum[:, :, None] - cum[:, None, :]
        decay = jnp.exp(jnp.where(mask, diff, -jnp.inf))
        scores = jnp.einsum('bthk,btshk,bshk->bhts', qi, decay, ki)
        o_intra = jnp.einsum('bhts,bshv->bthv', scores, vi.astype(jnp.float32))
        total = cum[:, -1]
        kdec = ki * jnp.exp(total[:, None] - cum)
        s_new = jnp.exp(total)[..., None] * s + jnp.einsum('bshk,bshv->bhkv', kdec, vi.astype(jnp.float32))
        return s_new, o_inter + o_intra

    s_fin, o = lax.scan(step, s0, (to_chunks(q), to_chunks(k), to_chunks(v), to_chunks(logf)))
    return jnp.moveaxis(o, 0, 1).reshape(bsz, t, h, dv), s_fin


def hgrn2_mixer(hx, hc, w_in, lb_logits, layer_j, norm_g, w_out, need_ctx):
    bsz = hx.shape[0]
    lb = jnp.cumsum(jax.nn.softmax(lb_logits.astype(jnp.float32), axis=1), axis=1)[:, layer_j]

    def project(h):
        q, f_fw, f_bw, v, g = jnp.split(h @ w_in, [HG_FDIM, 2 * HG_FDIM, 3 * HG_FDIM, 3 * HG_FDIM + HG_VDIM], axis=-1)
        f_fw = lb[0] + (1 - lb[0]) * jax.nn.sigmoid(f_fw.astype(jnp.float32))
        f_bw = lb[1] + (1 - lb[1]) * jax.nn.sigmoid(f_bw.astype(jnp.float32))
        return (_heads(jax.nn.silu(q), HG_HEADS), _heads(f_fw, HG_HEADS), _heads(f_bw, HG_HEADS),
                _heads(v, HG_HEADS), g)

    qc, ffc, fbc, vc, gc = project(hc)
    qx, ffx, fbx, vx, gx = project(hx)
    s0 = jnp.zeros((bsz, HG_HEADS, HG_DK, HG_DV), jnp.float32)
    o_c = 0.0
    o_x = 0.0
    for f_c, f_x, rev in ((ffc, ffx, False), (fbc, fbx, True)):
        flip = (lambda a: jnp.flip(a, 1)) if rev else (lambda a: a)
        oc_d, s_ctx = hgrn2_chunk_scan(flip(qc), flip(1 - f_c), flip(vc), flip(jnp.log(f_c)), s0)
        ox_d, _ = hgrn2_chunk_scan(flip(qx), flip(1 - f_x), flip(vx), flip(jnp.log(f_x)), s_ctx)
        o_c = o_c + flip(oc_d)
        o_x = o_x + flip(ox_d)

    def readout(o, g, h):
        o = o * lax.rsqrt(jnp.mean(o * o, -1, keepdims=True) + RMS_EPS) * norm_g
        o = o.reshape(o.shape[0], o.shape[1], HG_VDIM).astype(h.dtype) * jax.nn.silu(g)
        return o @ w_out

    yx = readout(o_x, gx, hx)
    yc = readout(o_c, gc, hc) if need_ctx else None
    return yx, yc


def centred_dwconv(u, w, b):
    width = u.shape[-1]
    out = lax.conv_general_dilated(u, w[:, None, :], window_strides=(1,),
                                   padding=[(CONV_PAD_LEFT, CONV_W - 1 - CONV_PAD_LEFT)],
                                   dimension_numbers=('NWC', 'WIO', 'NWC'), feature_group_count=width)
    return out + b


def linear_scan(a, b, h0):
    b = b.at[:, 0].add(a[:, 0] * h0)

    def combine(l, r):
        al, bl = l
        ar, br = r
        return al * ar, ar * bl + br

    _, h = lax.associative_scan(combine, (a, b), axis=1)
    return h


def rglru_mixer(hx, hc, w_in, conv_w, conv_b, w_a, b_a, w_x, b_x, lam, w_out, need_ctx):
    bsz, n, _ = hx.shape
    rows = n // GRID_W

    def to_cols(a):
        return a.reshape(bsz, rows, GRID_W, -1).transpose(0, 2, 1, 3).reshape(bsz, n, -1)

    def to_rows(a):
        return a.reshape(bsz, GRID_W, rows, -1).transpose(0, 2, 1, 3).reshape(bsz, n, -1)

    def branch(h):
        y, u = jnp.split(h @ w_in, 2, axis=-1)
        return jax.nn.gelu(y), centred_dwconv(u, conv_w, conv_b)

    def gates(u, dirn):
        t = u.shape[1]
        ub = u.reshape(bsz, t, LRU_HEADS, LRU_BLOCK)
        r = jax.nn.sigmoid((jnp.einsum('bthi,hij->bthj', ub, w_a[dirn]).reshape(bsz, t, LRU_WIDTH) + b_a[dirn]).astype(jnp.float32))
        ig = jax.nn.sigmoid((jnp.einsum('bthi,hij->bthj', ub, w_x[dirn]).reshape(bsz, t, LRU_WIDTH) + b_x[dirn]).astype(jnp.float32))
        log_a = -LRU_C * r * jax.nn.softplus(-lam[dirn].astype(jnp.float32))
        a = jnp.exp(log_a)
        bterm = jnp.sqrt(-jnp.expm1(2 * log_a)) * (ig * u.astype(jnp.float32))
        return a, bterm

    yc, uc = branch(hc)
    yx, ux = branch(to_cols(hx))
    h0 = jnp.zeros((bsz, LRU_WIDTH), jnp.float32)
    h_c = 0.0
    h_x = 0.0
    for dirn, rev in ((0, False), (1, True)):
        flip = (lambda a: jnp.flip(a, 1)) if rev else (lambda a: a)
        ac, bc = gates(uc, dirn)
        ax, bx = gates(ux, dirn)
        hcs = linear_scan(flip(ac), flip(bc), h0)
        hxs = linear_scan(flip(ax), flip(bx), hcs[:, -1])
        h_c = h_c + flip(hcs)
        h_x = h_x + flip(hxs)
    out_x = to_rows((yx * h_x.astype(yx.dtype)) @ w_out)
    out_c = ((yc * h_c.astype(yc.dtype)) @ w_out) if need_ctx else None
    return out_x, out_c


def moe(h, layer, w_router, b_router, w_gu, b_gu, w_down, b_down):
    t, d = h.shape
    logits = (h @ w_router[layer] + b_router[layer]).astype(jnp.float32)
    top_v, top_i = lax.top_k(logits, TOP_K)
    gate = jax.nn.softmax(top_v, axis=-1)
    n_pairs = t * TOP_K
    e_flat = top_i.reshape(-1).astype(jnp.int32)
    tok_flat = jnp.arange(n_pairs, dtype=jnp.int32) // TOP_K
    order = jnp.argsort(e_flat)
    e_sorted = e_flat[order]
    counts = jnp.zeros((N_EXPERTS,), jnp.int32).at[e_flat].add(1)
    padded = (counts + MOE_BLOCK - 1) // MOE_BLOCK * MOE_BLOCK
    start = jnp.cumsum(counts) - counts
    pend = jnp.cumsum(padded)
    pstart = pend - padded
    dest = pstart[e_sorted] + (jnp.arange(n_pairs, dtype=jnp.int32) - start[e_sorted])
    n_blocks = (n_pairs + N_EXPERTS * (MOE_BLOCK - 1) + MOE_BLOCK - 1) // MOE_BLOCK
    n_rows = n_blocks * MOE_BLOCK
    row_tok = jnp.full((n_rows,), t, jnp.int32).at[dest].set(tok_flat[order])
    row_w = jnp.zeros((n_rows,), jnp.float32).at[dest].set(gate.reshape(-1)[order])
    block_start = jnp.arange(n_blocks, dtype=jnp.int32) * MOE_BLOCK
    block_e = jnp.minimum(jnp.searchsorted(pend, block_start, side='right'), N_EXPERTS - 1)
    h_pad = jnp.concatenate([h, jnp.zeros((1, d), h.dtype)], 0)
    xb = h_pad[row_tok].reshape(n_blocks, MOE_BLOCK, d)

    def expert_block(args):
        xblk, e = args
        gu = xblk @ w_gu[layer, e] + b_gu[layer, e]
        g = jnp.minimum(gu[:, ::2], SWIGLU_LIMIT)
        u = jnp.clip(gu[:, 1::2], -SWIGLU_LIMIT, SWIGLU_LIMIT)
        act = (u + 1) * (g * jax.nn.sigmoid(SWIGLU_ALPHA * g))
        return act @ w_down[layer, e] + b_down[layer, e]

    yb = lax.map(expert_block, (xb, block_e)).reshape(n_rows, d)
    y = jnp.zeros((t + 1, d), yb.dtype).at[row_tok].add(yb * row_w[:, None].astype(yb.dtype))
    return y[:t]


def setup_inputs(seed: int = 0) -> dict:
    key = jax.random.key(seed)
    ks = iter(jax.random.split(key, 64))

    def nrm(shape, scale):
        return scale * jax.random.normal(next(ks), shape, jnp.float32)

    D = D_MODEL
    a0 = jax.random.uniform(next(ks), (N_LRU_LAYERS, 2, LRU_WIDTH), jnp.float32, 0.9, 0.999)
    s = a0 ** (1.0 / LRU_C)
    lam = jnp.log(s) - jnp.log1p(-s)
    return {
        'x': nrm((BATCH, SEQ, D), 1.0),
        'c': nrm((BATCH, D), 1.0),
        'ctx': nrm((BATCH, CTX_LEN, D), 1.0),
        'c_ctx': nrm((D,), 1.0),
        'ada_w': nrm((DEPTH, D, 6 * D), 0.5 * D ** -0.5),
        'ada_b': nrm((DEPTH, 6 * D), 0.02),
        'ln1_g': 1.0 + nrm((DEPTH, D), 0.02),
        'ln1_b': nrm((DEPTH, D), 0.02),
        'ln2_g': 1.0 + nrm((DEPTH, D), 0.02),
        'ln2_b': nrm((DEPTH, D), 0.02),
        'hg_w_in': nrm((N_HGRN_LAYERS, D, 3 * HG_FDIM + 2 * HG_VDIM), D ** -0.5),
        'hg_lb_logits': nrm((2, N_HGRN_LAYERS + 1, HG_FDIM), 0.5),
        'hg_norm_g': 1.0 + nrm((N_HGRN_LAYERS, HG_DV), 0.02),
        'hg_w_out': nrm((N_HGRN_LAYERS, HG_VDIM, D), DN_BETA * HG_VDIM ** -0.5),
        'lru_w_in': nrm((N_LRU_LAYERS, D, 2 * LRU_WIDTH), D ** -0.5),
        'lru_conv_w': nrm((N_LRU_LAYERS, CONV_W, LRU_WIDTH), CONV_W ** -0.5),
        'lru_conv_b': nrm((N_LRU_LAYERS, LRU_WIDTH), 0.02),
        'lru_w_a': nrm((N_LRU_LAYERS, 2, LRU_HEADS, LRU_BLOCK, LRU_BLOCK), LRU_BLOCK ** -0.5),
        'lru_b_a': nrm((N_LRU_LAYERS, 2, LRU_WIDTH), 0.02),
        'lru_w_x': nrm((N_LRU_LAYERS, 2, LRU_HEADS, LRU_BLOCK, LRU_BLOCK), LRU_BLOCK ** -0.5),
        'lru_b_x': nrm((N_LRU_LAYERS, 2, LRU_WIDTH), 0.02),
        'lru_lam': lam,
        'lru_w_out': nrm((N_LRU_LAYERS, LRU_WIDTH, D), DN_BETA * LRU_WIDTH ** -0.5),
        'moe_w_router': nrm((DEPTH, D, N_EXPERTS), D ** -0.5),
        'moe_b_router': nrm((DEPTH, N_EXPERTS), 0.01),
        'moe_w_gu': nrm((DEPTH, N_EXPERTS, D, 2 * D_EXPERT), D ** -0.5),
        'moe_b_gu': nrm((DEPTH, N_EXPERTS, 2 * D_EXPERT), 0.02),
        'moe_w_down': nrm((DEPTH, N_EXPERTS, D_EXPERT, D), DN_BETA * D_EXPERT ** -0.5),
        'moe_b_down': nrm((DEPTH, N_EXPERTS, D), 0.02),
    }


def reference(x, c, ctx, c_ctx, ada_w, ada_b, ln1_g, ln1_b, ln2_g, ln2_b,
              hg_w_in, hg_lb_logits, hg_norm_g, hg_w_out,
              lru_w_in, lru_conv_w, lru_conv_b, lru_w_a, lru_b_a, lru_w_x, lru_b_x, lru_lam, lru_w_out,
              moe_w_router, moe_b_router, moe_w_gu, moe_b_gu, moe_w_down, moe_b_down):
    bsz, n, d = x.shape
    n_ctx = ctx.shape[1]
    s_c = jax.nn.silu(c)
    s_cc = jax.nn.silu(c_ctx)
    for i in range(DEPTH):
        last = i == DEPTH - 1
        j = i // N_MIXERS
        mod_x = s_c @ ada_w[i] + ada_b[i]
        mod_c = s_cc @ ada_w[i] + ada_b[i]
        sh1, sc1, g1, sh2, sc2, g2 = jnp.split(mod_x[:, None, :], 6, axis=-1)
        csh1, csc1, cg1, csh2, csc2, cg2 = jnp.split(mod_c, 6, axis=-1)
        hx = modulate(x, sh1, sc1)
        hc = modulate(ctx, csh1, csc1)
        if i % N_MIXERS == 0:
            yx, yc = hgrn2_mixer(hx, hc, hg_w_in[j], hg_lb_logits, j, hg_norm_g[j], hg_w_out[j], not last)
        else:
            yx, yc = rglru_mixer(hx, hc, lru_w_in[j], lru_conv_w[j], lru_conv_b[j], lru_w_a[j], lru_b_a[j],
                                 lru_w_x[j], lru_b_x[j], lru_lam[j], lru_w_out[j], not last)
        x = layer_norm(DN_ALPHA * x + g1 * yx, ln1_g[i], ln1_b[i])
        hx = modulate(x, sh2, sc2).reshape(bsz * n, d)
        if last:
            yx = moe(hx, i, moe_w_router, moe_b_router, moe_w_gu, moe_b_gu, moe_w_down, moe_b_down)
        else:
            ctx = layer_norm(DN_ALPHA * ctx + cg1 * yc, ln1_g[i], ln1_b[i])
            hc = modulate(ctx, csh2, csc2).reshape(bsz * n_ctx, d)
            y = moe(jnp.concatenate([hx, hc], 0), i, moe_w_router, moe_b_router, moe_w_gu, moe_b_gu,
                    moe_w_down, moe_b_down)
            yx = y[:bsz * n]
            ctx = layer_norm(DN_ALPHA * ctx + cg2 * y[bsz * n:].reshape(bsz, n_ctx, d), ln2_g[i], ln2_b[i])
        x = layer_norm(DN_ALPHA * x + g2 * yx.reshape(bsz, n, d), ln2_g[i], ln2_b[i])
    return x
```

```python
import functools

import numpy as np
import jax
import jax.numpy as jnp
from jax import lax
from jax.experimental import pallas as pl
from jax.experimental.pallas import tpu as pltpu

F32 = jnp.float32
BF16 = jnp.bfloat16

D_MODEL = 2048
DEPTH = 2
GRID_W = 64
HG_HEADS = 16
HG_DK = 128
LRU_HEADS = 8
LRU_BLOCK = D_MODEL // LRU_HEADS
CONV_W = 4
LRU_C = 8.0
N_EXPERTS = 32
TOP_K = 4
SWIGLU_LIMIT = 7.0
SWIGLU_ALPHA = 1.702
DN_ALPHA = float((2 * DEPTH) ** 0.25)
LN_EPS = 1e-5
RMS_EPS = 1e-6
ROUTER_LANES = 128
ROUTER_PAD = -1e30

ROW_BLK = 256
MM_TM = 1024
MM_TN = 512
HG_CHUNK = 128
MOE_TM = 512
MOE_TN = 512
CMB_TM = 128
W_GROUP = 8
VMEM_LIMIT = 56 * 1024 * 1024


def _params(sem):
    return pltpu.CompilerParams(dimension_semantics=sem, vmem_limit_bytes=VMEM_LIMIT)


def _split_bf16(a):
    hi = a.astype(BF16)
    lo = (a - hi.astype(F32)).astype(BF16)
    return hi, lo


def _dot(a, b):
    return jnp.dot(a, b, preferred_element_type=F32)


def _dot3(a, b):
    ah, al = _split_bf16(a)
    bh, bl = _split_bf16(b)
    return _dot(ah, bh) + _dot(ah, bl) + _dot(al, bh)


def _dot_nt(a, b):
    return lax.dot_general(a, b, (((1,), (1,)), ((), ())), preferred_element_type=F32)


def _dot_tn(a, b):
    return lax.dot_general(a, b, (((0,), (0,)), ((), ())), preferred_element_type=F32)


def _sigmoid(x):
    return 1.0 / (1.0 + jnp.exp(-x))


def _ada_kernel(cv_ref, w_ref, b_ref, o_ref):
    x = cv_ref[...]
    s = x * _sigmoid(x)
    o_ref[...] = _dot3(s, w_ref[...]) + b_ref[...]


def _ada(cv, ada_w, ada_b):
    depth, d, n = ada_w.shape
    tn = 1024
    return pl.pallas_call(
        _ada_kernel,
        out_shape=jax.ShapeDtypeStruct((depth, 8, n), F32),
        grid=(depth, n // tn),
        in_specs=[
            pl.BlockSpec((8, d), lambda l, j: (0, 0)),
            pl.BlockSpec((None, d, tn), lambda l, j: (l, 0, j)),
            pl.BlockSpec((None, 1, tn), lambda l, j: (l, 0, j)),
        ],
        out_specs=pl.BlockSpec((None, 8, tn), lambda l, j: (l, 0, j)),
        compiler_params=_params(("arbitrary", "arbitrary")),
        name="ada_mod",
    )(cv, ada_w, ada_b.reshape(depth, 1, n))


def _mod_spec(section, n_x_blocks, ctx_row):
    def index(b, t):
        return (jnp.where(t < n_x_blocks, b, ctx_row), section, 0, 0)
    return pl.BlockSpec((None, None, 1, D_MODEL), index)


def _modulate_kernel(x_ref, sh_ref, sc_ref, o_ref):
    o_ref[...] = (x_ref[...] * (1.0 + sc_ref[...]) + sh_ref[...]).astype(o_ref.dtype)


def _modulate(xcat, mod, n_x_blocks, ctx_row):
    b, l, d = xcat.shape
    blk = pl.BlockSpec((None, ROW_BLK, d), lambda b_, t: (b_, t, 0))
    return pl.pallas_call(
        _modulate_kernel,
        out_shape=jax.ShapeDtypeStruct((b, l, d), BF16),
        grid=(b, l // ROW_BLK),
        in_specs=[blk, _mod_spec(0, n_x_blocks, ctx_row), _mod_spec(1, n_x_blocks, ctx_row)],
        out_specs=blk,
        compiler_params=_params(("parallel", "parallel")),
        name="modulate",
    )(xcat, mod, mod)


def _gelu_tanh(x):
    return 0.5 * x * (1.0 + jnp.tanh(0.7978845608028654 * (x + 0.044715 * (x * x * x))))


def _mm_kernel(a_ref, w_ref, *rest, mode, tiles_per_section):
    o_ref = rest[-1]
    acc = _dot(a_ref[...], w_ref[...].astype(BF16))
    sec = pl.program_id(1) // tiles_per_section
    if mode == "plain":
        o_ref[...] = acc
    elif mode == "hgrn_in":
        lb = rest[0][...]
        t = _sigmoid(acc)
        is_silu = (sec == 0) | (sec == 4)
        is_f = (sec == 1) | (sec == 2)
        o_ref[...] = jnp.where(is_silu, acc * t, jnp.where(is_f, lb + (1.0 - lb) * t, acc))
    elif mode == "lru_in":
        o_ref[...] = jnp.where(sec == 0, _gelu_tanh(acc), acc)
    else:
        raise ValueError(mode)


def _matmul(a, w, mode="plain", lb_row=None, section=None):
    m, k = a.shape
    n = w.shape[1]
    tm = MM_TM if m % MM_TM == 0 else ROW_BLK
    tn = MM_TN
    section = section or n
    in_specs = [
        pl.BlockSpec((tm, k), lambda i, j: (i, 0)),
        pl.BlockSpec((k, tn), lambda i, j: (0, j)),
    ]
    args = [a, w]
    if lb_row is not None:
        in_specs.append(pl.BlockSpec((1, tn), lambda i, j: (0, j)))
        args.append(lb_row)
    return pl.pallas_call(
        functools.partial(_mm_kernel, mode=mode, tiles_per_section=section // tn),
        out_shape=jax.ShapeDtypeStruct((m, n), F32),
        grid=(m // tm, n // tn),
        in_specs=in_specs,
        out_specs=pl.BlockSpec((tm, tn), lambda i, j: (i, j)),
        compiler_params=_params(("parallel", "arbitrary")),
        name="matmul_" + mode,
    )(*args)


def _hgrn_tables(c):
    levels = []
    bs = c
    while bs >= 2:
        levels.append(bs)
        bs //= 2
    nlev = len(levels)
    g = np.zeros((2, (2 + nlev) * c, c), np.float32)
    masks = np.zeros((2, nlev + 1, c, c), np.float32)
    r = np.arange(c)
    for d in range(2):
        for row in range(c):
            if d == 0:
                g[d, row, : row + 1] = 1.0
                g[d, c + row, row + 1:] = 1.0
            else:
                g[d, row, row:] = 1.0
                g[d, c + row, :row] = 1.0
        for li, bs in enumerate(levels):
            half = bs // 2
            for row in range(c):
                base = (row // bs) * bs
                mid = base + half
                pos = row - base
                blk = g[d, (2 + li) * c + row]
                if d == 0:
                    if pos >= half:
                        blk[mid: row + 1] = 1.0
                    else:
                        blk[row + 1: mid] = 1.0
                else:
                    if pos < half:
                        blk[row: mid] = 1.0
                    else:
                        blk[mid: row] = 1.0
            same = (r[:, None] // bs) == (r[None, :] // bs)
            t_hi = (r[:, None] % bs) >= half
            s_hi = (r[None, :] % bs) >= half
            if d == 0:
                masks[d, li] = same & t_hi & ~s_hi
            else:
                masks[d, li] = same & ~t_hi & s_hi
        masks[d, nlev] = np.eye(c)
    return jnp.asarray(g, BF16), jnp.asarray(masks, F32), nlev


def _hgrn_scan_kernel(q_ref, f_ref, v_ref, g_ref, m_ref, o_ref, st_ref, *, nlev):
    c = q_ref.shape[0]

    @pl.when(pl.program_id(2) == 0)
    def _():
        st_ref[...] = jnp.zeros_like(st_ref)

    gmat = g_ref[...]

    def head(h, carry):
        sl = pl.ds(pl.multiple_of(h * HG_DK, HG_DK), HG_DK)
        f = f_ref[:, sl]
        q = q_ref[:, sl]
        vb = v_ref[:, sl].astype(BF16)
        lf = jnp.log(f)
        k = 1.0 - f
        lf_hi, lf_lo = _split_bf16(lf)
        ex2 = _dot(gmat, jnp.concatenate([lf_hi, lf_lo], axis=1))
        ex = ex2[:, :HG_DK] + ex2[:, HG_DK:]
        e = jnp.exp(ex)
        st = st_ref[h]
        o = _dot_nt((q * e[0:c]).astype(BF16), st.astype(BF16))
        scores = m_ref[nlev] * _dot_nt(q.astype(BF16), k.astype(BF16))
        for li in range(nlev):
            el = e[(2 + li) * c:(3 + li) * c]
            scores = scores + m_ref[li] * _dot_nt((q * el).astype(BF16), (k * el).astype(BF16))
        o = o + _dot(scores.astype(BF16), vb)
        o_ref[:, sl] = o
        total = ex[0:1] + ex[c:c + 1]
        st_ref[h] = st * jnp.exp(total) + _dot_tn(vb, (k * e[c:2 * c]).astype(BF16))
        return carry

    lax.fori_loop(0, HG_HEADS, head, 0)


def _hgrn_scan(z, n_x):
    b, l, _ = z.shape
    c = HG_CHUNK
    nc = l // c
    ncx = n_x // c
    gmat, masks, nlev = _hgrn_tables(c)

    def chunk(d, s):
        fwd = jnp.where(s < nc - ncx, ncx + s, s - (nc - ncx))
        return jnp.where(d == 0, fwd, nc - 1 - s)

    def spec(col):
        return pl.BlockSpec((None, c, D_MODEL), lambda b_, d, s: (b_, chunk(d, s), col(d)))

    return pl.pallas_call(
        functools.partial(_hgrn_scan_kernel, nlev=nlev),
        out_shape=jax.ShapeDtypeStruct((2, b, l, D_MODEL), F32),
        grid=(b, 2, nc),
        in_specs=[
            spec(lambda d: 0),
            spec(lambda d: 1 + d),
            spec(lambda d: 3),
            pl.BlockSpec((None,) + gmat.shape[1:], lambda b_, d, s: (d, 0, 0)),
            pl.BlockSpec((None,) + masks.shape[1:], lambda b_, d, s: (d, 0, 0, 0)),
        ],
        out_specs=pl.BlockSpec((None, None, c, D_MODEL), lambda b_, d, s: (d, b_, chunk(d, s), 0)),
        scratch_shapes=[pltpu.VMEM((HG_HEADS, HG_DK, HG_DK), F32)],
        compiler_params=_params(("parallel", "arbitrary", "arbitrary")),
        name="hgrn_scan",
    )(z, z, z, gmat, masks)


def _hgrn_prep_kernel(of_ref, ob_ref, g_ref, ng_ref, o_ref):
    ng = ng_ref[...]
    for h in range(HG_HEADS):
        sl = slice(h * HG_DK, (h + 1) * HG_DK)
        o = of_ref[:, sl] + ob_ref[:, sl]
        ms = jnp.mean(o * o, axis=-1, keepdims=True)
        y = o * lax.rsqrt(ms + RMS_EPS) * ng
        o_ref[:, sl] = (y * g_ref[:, sl]).astype(o_ref.dtype)


def _hgrn_prep(o2, z, norm_g):
    _, m, d = o2.shape
    tm = 512 if m % 512 == 0 else ROW_BLK
    return pl.pallas_call(
        _hgrn_prep_kernel,
        out_shape=jax.ShapeDtypeStruct((m, d), BF16),
        grid=(m // tm,),
        in_specs=[
            pl.BlockSpec((None, tm, d), lambda i: (0, i, 0)),
            pl.BlockSpec((None, tm, d), lambda i: (1, i, 0)),
            pl.BlockSpec((tm, d), lambda i: (i, 4)),
            pl.BlockSpec((1, HG_DK), lambda i: (0, 0)),
        ],
        out_specs=pl.BlockSpec((tm, d), lambda i: (i, 0)),
        compiler_params=_params(("parallel",)),
        name="hgrn_prep",
    )(o2, o2, z, norm_g.reshape(1, HG_DK))


def _layer_norm(v, g, b):
    mu = jnp.mean(v, axis=-1, keepdims=True)
    vc = v - mu
    var = jnp.mean(vc * vc, axis=-1, keepdims=True)
    return vc * lax.rsqrt(var + LN_EPS) * g + b


def _router_topk(h, wr, br, ti_ref, tg_ref):
    logits = _dot3(h, wr) + br
    lane = lax.broadcasted_iota(jnp.int32, logits.shape, 1)
    lane_f = lane.astype(F32)
    vals, ids = [], []
    cur = logits
    for _ in range(TOP_K):
        mx = jnp.max(cur, axis=-1, keepdims=True)
        idx = jnp.min(jnp.where(cur == mx, lane_f, float(ROUTER_LANES)), axis=-1, keepdims=True)
        vals.append(mx)
        ids.append(idx)
        cur = jnp.where(lane_f == idx, -jnp.inf, cur)
    ex = [jnp.exp(v - vals[0]) for v in vals]
    den = ex[0] + ex[1] + ex[2] + ex[3]
    ti = jnp.zeros(logits.shape, F32)
    tg = jnp.zeros(logits.shape, F32)
    for kk in range(TOP_K):
        ti = jnp.where(lane == kk, ids[kk], ti)
        tg = jnp.where(lane == kk, ex[kk] / den, tg)
    ti_ref[...] = ti.astype(jnp.int32)
    tg_ref[...] = tg


def _post_kernel(x_ref, y_ref, gate_ref, lng_ref, lnb_ref, sh_ref, sc_ref, wr_ref, br_ref,
                 xo_ref, h_ref, ti_ref, tg_ref):
    v = DN_ALPHA * x_ref[...] + gate_ref[...] * y_ref[...]
    xn = _layer_norm(v, lng_ref[...], lnb_ref[...])
    xo_ref[...] = xn
    h = xn * (1.0 + sc_ref[...]) + sh_ref[...]
    h_ref[...] = h
    _router_topk(h, wr_ref[...], br_ref[...], ti_ref, tg_ref)


def _post(x, y, mod, ln_g, ln_b, w_router, b_router, n_blocks, n_x_blocks, ctx_row):
    b, _, d = x.shape
    rows = n_blocks * ROW_BLK
    blk = pl.BlockSpec((None, ROW_BLK, d), lambda b_, t: (b_, t, 0))
    lane_blk = pl.BlockSpec((None, ROW_BLK, 128), lambda b_, t: (b_, t, 0))
    row = pl.BlockSpec((1, d), lambda b_, t: (0, 0))
    ms = functools.partial(_mod_spec, n_x_blocks=n_x_blocks, ctx_row=ctx_row)
    wr = jnp.zeros((d, ROUTER_LANES), F32).at[:, :N_EXPERTS].set(w_router)
    br = jnp.full((1, ROUTER_LANES), ROUTER_PAD, F32).at[0, :N_EXPERTS].set(b_router)
    return pl.pallas_call(
        _post_kernel,
        out_shape=(
            jax.ShapeDtypeStruct((b, rows, d), F32),
            jax.ShapeDtypeStruct((b, rows, d), F32),
            jax.ShapeDtypeStruct((b, rows, 128), jnp.int32),
            jax.ShapeDtypeStruct((b, rows, 128), F32),
        ),
        grid=(b, n_blocks),
        in_specs=[blk, blk, ms(2), row, row, ms(3), ms(4),
                  pl.BlockSpec((d, ROUTER_LANES), lambda b_, t: (0, 0)),
                  pl.BlockSpec((1, ROUTER_LANES), lambda b_, t: (0, 0))],
        out_specs=(blk, blk, lane_blk, lane_blk),
        compiler_params=_params(("parallel", "parallel")),
        name="post_mixer",
    )(x, y, mod, ln_g.reshape(1, d), ln_b.reshape(1, d), mod, mod, wr, br)


def _gather_kernel(nu_ref, idx_ref, h_hbm, o_ref, buf, sem):
    i = pl.program_id(0)
    tm = buf.shape[0]

    @pl.when(i < nu_ref[0])
    def _():
        base = i * tm

        def start(r, carry):
            tok = idx_ref[base + r]
            pltpu.make_async_copy(h_hbm.at[pl.ds(tok, 1), :], buf.at[pl.ds(r, 1), :], sem).start()
            return carry

        lax.fori_loop(0, tm, start, 0, unroll=8)

        def wait(r, carry):
            pltpu.make_async_copy(h_hbm.at[pl.ds(0, 1), :], buf.at[pl.ds(r, 1), :], sem).wait()
            return carry

        lax.fori_loop(0, tm, wait, 0, unroll=8)
        o_ref[...] = buf[...].astype(o_ref.dtype)

    @pl.when(i >= nu_ref[0])
    def _():
        o_ref[...] = jnp.zeros_like(o_ref)


def _moe_gather(h, row_tok, n_used, n_blocks):
    t, d = h.shape
    tm = MOE_TM
    return pl.pallas_call(
        _gather_kernel,
        out_shape=jax.ShapeDtypeStruct((n_blocks * tm, d), BF16),
        grid_spec=pltpu.PrefetchScalarGridSpec(
            num_scalar_prefetch=2,
            grid=(n_blocks,),
            in_specs=[pl.BlockSpec(memory_space=pl.ANY)],
            out_specs=pl.BlockSpec((tm, d), lambda i, nu, idx: (i, 0)),
            scratch_shapes=[pltpu.VMEM((tm, d), F32), pltpu.SemaphoreType.DMA],
        ),
        compiler_params=_params(("arbitrary",)),
        name="moe_gather",
    )(n_used, row_tok, h)


def _expert_changed(be_ref, i):
    return (i == 0) | (be_ref[i] != be_ref[jnp.maximum(i - 1, 0)])


def _moe_gu_kernel(be_ref, nu_ref, x_ref, w_ref, b_ref, sel_ref, o_ref, wbf):
    i = pl.program_id(1)
    tn = w_ref.shape[1]

    @pl.when(i < nu_ref[0])
    def _():
        @pl.when(_expert_changed(be_ref, i))
        def _():
            wbf[...] = w_ref[...].astype(BF16)

        gu = _dot(x_ref[...], wbf[...]) + b_ref[...]
        g = jnp.minimum(gu, SWIGLU_LIMIT)
        u = jnp.clip(gu, -SWIGLU_LIMIT, SWIGLU_LIMIT)
        glu = g * _sigmoid(SWIGLU_ALPHA * g)
        act = glu * pltpu.roll(u + 1.0, tn - 1, 1)
        o_ref[...] = _dot(act.astype(BF16), sel_ref[...]).astype(o_ref.dtype)

    @pl.when(i >= nu_ref[0])
    def _():
        o_ref[...] = jnp.zeros_like(o_ref)


def _moe_down_kernel(be_ref, nu_ref, a_ref, w_ref, b_ref, o_ref, wbf):
    i = pl.program_id(1)

    @pl.when(i < nu_ref[0])
    def _():
        @pl.when(_expert_changed(be_ref, i))
        def _():
            wbf[...] = w_ref[...].astype(BF16)

        o_ref[...] = _dot(a_ref[...], wbf[...]) + b_ref[...]

    @pl.when(i >= nu_ref[0])
    def _():
        o_ref[...] = jnp.zeros_like(o_ref)


def _moe_experts(xs, block_e, n_used, layer, w_gu, b_gu, w_down, b_down):
    n_rows, d = xs.shape
    tm, tn = MOE_TM, MOE_TN
    n_blocks = n_rows // tm
    de2 = w_gu.shape[-1]
    de = de2 // 2
    sel = np.zeros((tn, tn // 2), np.float32)
    sel[2 * np.arange(tn // 2), np.arange(tn // 2)] = 1.0
    sel = jnp.asarray(sel, BF16)

    def row(i, nu):
        return jnp.minimum(i, nu[0] - 1)

    act = pl.pallas_call(
        _moe_gu_kernel,
        out_shape=jax.ShapeDtypeStruct((n_rows, de), BF16),
        grid_spec=pltpu.PrefetchScalarGridSpec(
            num_scalar_prefetch=2,
            grid=(de2 // tn, n_blocks),
            in_specs=[
                pl.BlockSpec((tm, d), lambda j, i, be, nu: (row(i, nu), 0)),
                pl.BlockSpec((None, None, d, tn), lambda j, i, be, nu: (layer, be[i], 0, j)),
                pl.BlockSpec((None, None, 1, tn), lambda j, i, be, nu: (layer, be[i], 0, j)),
                pl.BlockSpec((tn, tn // 2), lambda j, i, be, nu: (0, 0)),
            ],
            out_specs=pl.BlockSpec((tm, tn // 2), lambda j, i, be, nu: (i, j)),
            scratch_shapes=[pltpu.VMEM((d, tn), BF16)],
        ),
        compiler_params=_params(("arbitrary", "arbitrary")),
        name="moe_gate_up",
    )(block_e, n_used, xs, w_gu, b_gu.reshape(DEPTH, N_EXPERTS, 1, de2), sel)

    yb = pl.pallas_call(
        _moe_down_kernel,
        out_shape=jax.ShapeDtypeStruct((n_rows, d), F32),
        grid_spec=pltpu.PrefetchScalarGridSpec(
            num_scalar_prefetch=2,
            grid=(d // tn, n_blocks),
            in_specs=[
                pl.BlockSpec((tm, de), lambda j, i, be, nu: (row(i, nu), 0)),
                pl.BlockSpec((None, None, de, tn), lambda j, i, be, nu: (layer, be[i], 0, j)),
                pl.BlockSpec((None, None, 1, tn), lambda j, i, be, nu: (layer, be[i], 0, j)),
            ],
            out_specs=pl.BlockSpec((tm, tn), lambda j, i, be, nu: (i, j)),
            scratch_shapes=[pltpu.VMEM((de, tn), BF16)],
        ),
        compiler_params=_params(("arbitrary", "arbitrary")),
        name="moe_down",
    )(block_e, n_used, act, w_down, b_down.reshape(DEPTH, N_EXPERTS, 1, d))
    return yb


def _combine_kernel(dest_ref, tg_ref, x_ref, gate_ref, lng_ref, lnb_ref, *rest, tokens_per_batch,
                    with_next):
    if with_next:
        sh_ref, sc_ref, yb_hbm, xo_ref, h_ref, buf, sem = rest
    else:
        yb_hbm, xo_ref, buf, sem = rest
    tmc = buf.shape[1]
    base = (pl.program_id(0) * tokens_per_batch + pl.program_id(1) * tmc) * TOP_K

    def start(t, carry):
        for kk in range(TOP_K):
            row = dest_ref[base + t * TOP_K + kk]
            pltpu.make_async_copy(yb_hbm.at[pl.ds(row, 1), :], buf.at[kk, pl.ds(t, 1), :], sem).start()
        return carry

    lax.fori_loop(0, tmc, start, 0, unroll=4)

    def wait(t, carry):
        for kk in range(TOP_K):
            pltpu.make_async_copy(yb_hbm.at[pl.ds(0, 1), :], buf.at[kk, pl.ds(t, 1), :], sem).wait()
        return carry

    lax.fori_loop(0, tmc, wait, 0, unroll=4)

    tg = tg_ref[...]
    y = tg[:, 0:1] * buf[0]
    for kk in range(1, TOP_K):
        y = y + tg[:, kk:kk + 1] * buf[kk]
    v = DN_ALPHA * x_ref[...] + gate_ref[...] * y
    xn = _layer_norm(v, lng_ref[...], lnb_ref[...])
    xo_ref[...] = xn
    if with_next:
        h_ref[...] = (xn * (1.0 + sc_ref[...]) + sh_ref[...]).astype(h_ref.dtype)


def _combine(x, yb, dest, tg, mod, ln_g, ln_b, n_x_blocks, ctx_row, mod_next=None):
    b, l, d = x.shape
    tmc = CMB_TM
    per = ROW_BLK // tmc
    with_next = mod_next is not None

    def mod_spec(section):
        def index(b_, t, dest_):
            return (jnp.where(t // per < n_x_blocks, b_, ctx_row), section, 0, 0)
        return pl.BlockSpec((None, None, 1, d), index)

    blk = pl.BlockSpec((None, tmc, d), lambda b_, t, dest_: (b_, t, 0))
    row = pl.BlockSpec((1, d), lambda b_, t, dest_: (0, 0))
    in_specs = [pl.BlockSpec((None, tmc, 128), lambda b_, t, dest_: (b_, t, 0)), blk, mod_spec(5), row, row]
    args = [tg, x, mod, ln_g.reshape(1, d), ln_b.reshape(1, d)]
    out_shape = [jax.ShapeDtypeStruct((b, l, d), F32)]
    out_specs = [blk]
    if with_next:
        in_specs += [mod_spec(0), mod_spec(1)]
        args += [mod_next, mod_next]
        out_shape.append(jax.ShapeDtypeStruct((b, l, d), BF16))
        out_specs.append(blk)
    in_specs.append(pl.BlockSpec(memory_space=pl.ANY))
    args.append(yb)
    return pl.pallas_call(
        functools.partial(_combine_kernel, tokens_per_batch=l, with_next=with_next),
        out_shape=tuple(out_shape),
        grid_spec=pltpu.PrefetchScalarGridSpec(
            num_scalar_prefetch=1,
            grid=(b, l // tmc),
            in_specs=in_specs,
            out_specs=tuple(out_specs),
            scratch_shapes=[pltpu.VMEM((TOP_K, tmc, d), F32), pltpu.SemaphoreType.DMA],
        ),
        compiler_params=_params(("arbitrary", "arbitrary")),
        name="moe_combine",
    )(dest, *args)


def _moe_plan(top_i):
    t = top_i.shape[0]
    n_pairs = t * TOP_K
    tm = MOE_TM
    e_flat = top_i.reshape(-1)
    onehot = (e_flat[:, None] == jnp.arange(N_EXPERTS, dtype=jnp.int32)[None, :]).astype(jnp.int32)
    csum = jnp.cumsum(onehot, axis=0)
    rank = jnp.take_along_axis(csum, e_flat[:, None], axis=1)[:, 0] - 1
    counts = csum[-1]
    padded = (counts + tm - 1) // tm * tm
    pend = jnp.cumsum(padded)
    pstart = pend - padded
    dest = (pstart[e_flat] + rank).astype(jnp.int32)
    n_blocks = (n_pairs + N_EXPERTS * (tm - 1) + tm - 1) // tm
    n_rows = n_blocks * tm
    tok = jnp.arange(n_pairs, dtype=jnp.int32) // TOP_K
    row_tok = jnp.zeros((n_rows,), jnp.int32).at[dest].set(tok)
    n_used = (pend[-1] // tm).astype(jnp.int32)
    blk = jnp.arange(n_blocks, dtype=jnp.int32)
    block_e = jnp.searchsorted(pend, jnp.minimum(blk, n_used - 1) * tm, side="right")
    block_e = jnp.minimum(block_e, N_EXPERTS - 1).astype(jnp.int32)
    return dest, row_tok, block_e, n_used.reshape(1), n_blocks


def _moe(h, top_i, layer, w_gu, b_gu, w_down, b_down):
    dest, row_tok, block_e, n_used, n_blocks = _moe_plan(top_i)
    xs = _moe_gather(h, row_tok, n_used, n_blocks)
    yb = _moe_experts(xs, block_e, n_used, layer, w_gu, b_gu, w_down, b_down)
    return yb, dest


def _conv_ctx_kernel(u_ref, w_ref, b_ref, o_ref):
    u = u_ref[...]
    n = u.shape[0]
    row = lax.broadcasted_iota(jnp.int32, u.shape, 0)
    w = w_ref[...]
    um2 = jnp.where(row >= 2, pltpu.roll(u, 2, 0), 0.0)
    um1 = jnp.where(row >= 1, pltpu.roll(u, 1, 0), 0.0)
    up1 = jnp.where(row < n - 1, pltpu.roll(u, n - 1, 0), 0.0)
    o_ref[...] = w[0:1] * um2 + w[1:2] * um1 + w[2:3] * u + w[3:4] * up1 + b_ref[...]


def _conv_ctx(z1, conv_w, conv_b, n_x, n_ctx):
    b, l, _ = z1.shape
    w = conv_w.shape[1]
    return pl.pallas_call(
        _conv_ctx_kernel,
        out_shape=jax.ShapeDtypeStruct((b, n_ctx, w), F32),
        grid=(b,),
        in_specs=[
            pl.BlockSpec((None, n_ctx, w), lambda b_: (b_, n_x // n_ctx, 1)),
            pl.BlockSpec((CONV_W, w), lambda b_: (0, 0)),
            pl.BlockSpec((1, w), lambda b_: (0, 0)),
        ],
        out_specs=pl.BlockSpec((None, n_ctx, w), lambda b_: (b_, 0, 0)),
        compiler_params=_params(("parallel",)),
        name="lru_conv_ctx",
    )(z1, conv_w, conv_b.reshape(1, w))


def _conv_x_kernel(u_ref, prev_ref, next_ref, w_ref, b_ref, o_ref):
    g = pl.program_id(1)
    ng = pl.num_programs(1)
    r = u_ref.shape[0]
    wg = u_ref.shape[1]
    w = w_ref[...]
    c0, c1, c2, c3 = w[0:1], w[1:2], w[2:3], w[3:4]
    bias = b_ref[...]
    sub = lax.broadcasted_iota(jnp.int32, u_ref.shape[1:], 0)

    def from_prev_col(x, halo):
        return jnp.where(sub == 0, halo, pltpu.roll(x, 1, 0))

    def from_next_col(x, halo):
        return jnp.where(sub == wg - 1, halo, pltpu.roll(x, wg - 1, 0))

    has_prev = (g > 0).astype(F32)
    has_next = (g < ng - 1).astype(F32)
    hp = prev_ref.shape[0]
    u_m1 = from_prev_col(u_ref[r - 1], prev_ref[hp - 1, wg - 1:wg, :] * has_prev)
    u_m2 = from_prev_col(u_ref[r - 2], prev_ref[hp - 2, wg - 1:wg, :] * has_prev)
    u_p = from_next_col(u_ref[0], next_ref[0, 0:1, :] * has_next)

    def interior(i, carry):
        o_ref[i] = c0 * u_ref[i - 2] + c1 * u_ref[i - 1] + c2 * u_ref[i] + c3 * u_ref[i + 1] + bias
        return carry

    lax.fori_loop(2, r - 1, interior, 0)
    o_ref[0] = c0 * u_m2 + c1 * u_m1 + c2 * u_ref[0] + c3 * u_ref[1] + bias
    o_ref[1] = c0 * u_m1 + c1 * u_ref[0] + c2 * u_ref[1] + c3 * u_ref[2] + bias
    o_ref[r - 1] = c0 * u_ref[r - 3] + c1 * u_ref[r - 2] + c2 * u_ref[r - 1] + c3 * u_p + bias


def _conv_x(z1, conv_w, conv_b, n_x):
    b, l, f = z1.shape
    w = conv_w.shape[1]
    rows = n_x // GRID_W
    wg = W_GROUP
    ng = GRID_W // wg
    z4 = z1.reshape(b, l // GRID_W, GRID_W, f)
    halo = 8
    last_halo = rows // halo - 1
    return pl.pallas_call(
        _conv_x_kernel,
        out_shape=jax.ShapeDtypeStruct((b, rows, GRID_W, w), F32),
        grid=(b, ng),
        in_specs=[
            pl.BlockSpec((None, rows, wg, w), lambda b_, g: (b_, 0, g, 1)),
            pl.BlockSpec((None, halo, wg, w), lambda b_, g: (b_, last_halo, jnp.maximum(g - 1, 0), 1)),
            pl.BlockSpec((None, halo, wg, w), lambda b_, g: (b_, 0, jnp.minimum(g + 1, ng - 1), 1)),
            pl.BlockSpec((CONV_W, w), lambda b_, g: (0, 0)),
            pl.BlockSpec((1, w), lambda b_, g: (0, 0)),
        ],
        out_specs=pl.BlockSpec((None, rows, wg, w), lambda b_, g: (b_, 0, g, 0)),
        compiler_params=_params(("parallel", "parallel")),
        name="lru_conv_x",
    )(z4, z4, z4, conv_w, conv_b.reshape(1, w))


def _gates_kernel(u_ref, wa_ref, wx_ref, ba_ref, bx_ref, lam_ref, a_ref, b_ref, wabf, wxbf):
    @pl.when(pl.program_id(0) == 0)
    def _():
        wabf[...] = wa_ref[...].astype(BF16)
        wxbf[...] = wx_ref[...].astype(BF16)

    for d in range(2):
        lam = lam_ref[d]
        nl = -lam
        softplus = jnp.maximum(nl, 0.0) + jnp.log(1.0 + jnp.exp(-jnp.abs(nl)))
        for h in range(LRU_HEADS):
            sl = slice(h * LRU_BLOCK, (h + 1) * LRU_BLOCK)
            u = u_ref[:, sl]
            ub = u.astype(BF16)
            r = _sigmoid(_dot(ub, wabf[d, h]) + ba_ref[d][:, sl])
            ig = _sigmoid(_dot(ub, wxbf[d, h]) + bx_ref[d][:, sl])
            log_a = (-LRU_C) * r * softplus[:, sl]
            a = jnp.exp(log_a)
            a_ref[d, :, sl] = a
            b_ref[d, :, sl] = jnp.sqrt(1.0 - a * a) * (ig * u)


def _gates(cu, w_a, w_x, b_a, b_x, lam):
    m, w = cu.shape
    tm = ROW_BLK
    full5 = pl.BlockSpec(w_a.shape, lambda i: (0, 0, 0, 0))
    vec = pl.BlockSpec((2, 1, w), lambda i: (0, 0, 0))
    out = pl.BlockSpec((2, tm, w), lambda i: (0, i, 0))
    return pl.pallas_call(
        _gates_kernel,
        out_shape=(jax.ShapeDtypeStruct((2, m, w), F32), jax.ShapeDtypeStruct((2, m, w), F32)),
        grid=(m // tm,),
        in_specs=[pl.BlockSpec((tm, w), lambda i: (i, 0)), full5, full5, vec, vec, vec],
        out_specs=(out, out),
        scratch_shapes=[pltpu.VMEM(w_a.shape, BF16), pltpu.VMEM(w_a.shape, BF16)],
        compiler_params=_params(("arbitrary",)),
        name="lru_gates",
    )(cu, w_a, w_x, b_a.reshape(2, 1, w), b_x.reshape(2, 1, w), lam.reshape(2, 1, w))


def _lru_scan_kernel(ax_ref, bx_ref, ac_ref, bc_ref, h_ref, carry, acum, *, rev):
    s = pl.program_id(2)
    rows = ax_ref.shape[0]
    wg = ax_ref.shape[1]
    n_ctx = ac_ref.shape[0]
    ft = ax_ref.shape[2]

    @pl.when(s == 0)
    def _():
        def body(t, h):
            tt = n_ctx - 1 - t if rev else t
            return ac_ref[pl.ds(tt, 1), :] * h + bc_ref[pl.ds(tt, 1), :]

        carry[...] = lax.fori_loop(0, n_ctx, body, jnp.zeros((1, ft), F32))

    @pl.when(s > 0)
    def _():
        def seg(i, c):
            a_run, h_run = c
            r = rows - 1 - i if rev else i
            a = ax_ref[r]
            h_run = a * h_run + bx_ref[r]
            a_run = a_run * a
            h_ref[r] = h_run
            acum[r] = a_run
            return a_run, h_run

        a_tot, h_tot = lax.fori_loop(
            0, rows, seg, (jnp.ones((wg, ft), F32), jnp.zeros((wg, ft), F32)))
        h = carry[...]
        sub = lax.broadcasted_iota(jnp.int32, (wg, ft), 0)
        h_in = jnp.zeros((wg, ft), F32)
        for j in (range(wg - 1, -1, -1) if rev else range(wg)):
            h_in = jnp.where(sub == j, h, h_in)
            h = a_tot[j:j + 1] * h + h_tot[j:j + 1]
        carry[...] = h

        def fix(r, c):
            h_ref[r] = h_ref[r] + acum[r] * h_in
            return c

        lax.fori_loop(0, rows, fix, 0)


def _lru_scan(a_x, b_x, a_c, b_c, d, rev):
    _, b, rows, gw, w = a_x.shape
    n_ctx = a_c.shape[2]
    wg = W_GROUP
    ng = gw // wg
    ft = 1024

    def grp(s):
        k = jnp.maximum(s - 1, 0)
        return ng - 1 - k if rev else k

    xblk = pl.BlockSpec((None, None, rows, wg, ft), lambda b_, f, s: (d, b_, 0, grp(s), f))
    cblk = pl.BlockSpec((None, None, n_ctx, ft), lambda b_, f, s: (d, b_, 0, f))
    return pl.pallas_call(
        functools.partial(_lru_scan_kernel, rev=rev),
        out_shape=jax.ShapeDtypeStruct((b, rows, gw, w), F32),
        grid=(b, w // ft, ng + 1),
        in_specs=[xblk, xblk, cblk, cblk],
        out_specs=pl.BlockSpec((None, rows, wg, ft), lambda b_, f, s: (b_, 0, grp(s), f)),
        scratch_shapes=[pltpu.VMEM((1, ft), F32), pltpu.VMEM((rows, wg, ft), F32)],
        compiler_params=_params(("parallel", "parallel", "arbitrary")),
        name="lru_scan_bwd" if rev else "lru_scan_fwd",
    )(a_x, b_x, a_c, b_c)


def _lru_prep_kernel(y_ref, hf_ref, hb_ref, o_ref):
    o_ref[...] = (y_ref[...] * (hf_ref[...] + hb_ref[...])).astype(o_ref.dtype)


def _lru_prep(z1, h_f, h_b, n_x):
    b, l, _ = z1.shape
    w = h_f.shape[-1]
    blk = pl.BlockSpec((None, ROW_BLK, w), lambda b_, t: (b_, t, 0))
    return pl.pallas_call(
        _lru_prep_kernel,
        out_shape=jax.ShapeDtypeStruct((b, n_x, w), BF16),
        grid=(b, n_x // ROW_BLK),
        in_specs=[blk, blk, blk],
        out_specs=blk,
        compiler_params=_params(("parallel", "parallel")),
        name="lru_prep",
    )(z1, h_f, h_b)


def kernel(x, c, ctx, c_ctx, ada_w, ada_b, ln1_g, ln1_b, ln2_g, ln2_b, hg_w_in, hg_lb_logits, hg_norm_g, hg_w_out, lru_w_in, lru_conv_w, lru_conv_b, lru_w_a, lru_b_a, lru_w_x, lru_b_x, lru_lam, lru_w_out, moe_w_router, moe_b_router, moe_w_gu, moe_b_gu, moe_w_down, moe_b_down):
    bsz, n_x, d = x.shape
    n_ctx = ctx.shape[1]
    assert d == D_MODEL and ada_w.shape[0] == DEPTH == 2
    assert n_x % ROW_BLK == 0 and n_ctx == ROW_BLK and n_x % (GRID_W * 8) == 0 and bsz < 8
    l = n_x + n_ctx
    n_x_blocks = n_x // ROW_BLK
    n_blocks = l // ROW_BLK
    ctx_row = bsz

    cv = jnp.zeros((8, d), F32).at[:bsz].set(c).at[bsz].set(c_ctx)
    mod = _ada(cv, ada_w, ada_b).reshape(DEPTH, 8, 6, 1, d)
    xcat = jnp.concatenate([x, ctx], axis=1)

    h0 = _modulate(xcat, mod[0], n_x_blocks, ctx_row)
    lb = jnp.cumsum(jax.nn.softmax(hg_lb_logits.astype(F32), axis=1), axis=1)[:, 0]
    lb_row = jnp.zeros((1, 5 * d), F32).at[0, d:3 * d].set(lb.reshape(-1))
    z = _matmul(h0.reshape(bsz * l, d), hg_w_in[0], mode="hgrn_in", lb_row=lb_row, section=d)
    o2 = _hgrn_scan(z.reshape(bsz, l, 5 * d), n_x)
    a0 = _hgrn_prep(o2.reshape(2, bsz * l, d), z, hg_norm_g[0])
    y0 = _matmul(a0, hg_w_out[0]).reshape(bsz, l, d)
    x1, h1, ti, tg = _post(xcat, y0, mod[0], ln1_g[0], ln1_b[0], moe_w_router[0], moe_b_router[0],
                           n_blocks, n_x_blocks, ctx_row)
    yb, dest = _moe(h1.reshape(bsz * l, d), ti.reshape(bsz * l, 128)[:, :TOP_K], 0,
                    moe_w_gu, moe_b_gu, moe_w_down, moe_b_down)
    x2, h2 = _combine(x1, yb, dest, tg, mod[0], ln2_g[0], ln2_b[0], n_x_blocks, ctx_row, mod_next=mod[1])

    z1 = _matmul(h2.reshape(bsz * l, d), lru_w_in[0], mode="lru_in", section=d).reshape(bsz, l, 2 * d)
    cu_x = _conv_x(z1, lru_conv_w[0], lru_conv_b[0], n_x)
    cu_c = _conv_ctx(z1, lru_conv_w[0], lru_conv_b[0], n_x, n_ctx)
    rows = n_x // GRID_W
    a_x, b_x = _gates(cu_x.reshape(bsz * n_x, d), lru_w_a[0], lru_w_x[0], lru_b_a[0], lru_b_x[0], lru_lam[0])
    a_c, b_c = _gates(cu_c.reshape(bsz * n_ctx, d), lru_w_a[0], lru_w_x[0], lru_b_a[0], lru_b_x[0], lru_lam[0])
    a_x = a_x.reshape(2, bsz, rows, GRID_W, d)
    b_x = b_x.reshape(2, bsz, rows, GRID_W, d)
    a_c = a_c.reshape(2, bsz, n_ctx, d)
    b_c = b_c.reshape(2, bsz, n_ctx, d)
    h_f = _lru_scan(a_x, b_x, a_c, b_c, 0, rev=False).reshape(bsz, n_x, d)
    h_b = _lru_scan(a_x, b_x, a_c, b_c, 1, rev=True).reshape(bsz, n_x, d)
    a1 = _lru_prep(z1, h_f, h_b, n_x)
    y1 = _matmul(a1.reshape(bsz * n_x, d), lru_w_out[0]).reshape(bsz, n_x, d)
    x3, h3, ti, tg = _post(x2, y1, mod[1], ln1_g[1], ln1_b[1], moe_w_router[1], moe_b_router[1],
                           n_x_blocks, n_x_blocks, ctx_row)
    yb, dest = _moe(h3.reshape(bsz * n_x, d), ti.reshape(bsz * n_x, 128)[:, :TOP_K], 1,
                    moe_w_gu, moe_b_gu, moe_w_down, moe_b_down)
    (out,) = _combine(x3, yb, dest, tg, mod[1], ln2_g[1], ln2_b[1], n_x_blocks, ctx_row)
    return out
```

```python
import functools

import numpy as np
import jax
import jax.numpy as jnp
from jax import lax
from jax.experimental import pallas as pl
from jax.experimental.pallas import tpu as pltpu

F32 = jnp.float32
BF16 = jnp.bfloat16

D_MODEL = 2048
DEPTH = 2
GRID_W = 64
HG_HEADS = 16
HG_DK = 128
LRU_HEADS = 8
LRU_BLOCK = D_MODEL // LRU_HEADS
CONV_W = 4
LRU_C = 8.0
N_EXPERTS = 32
TOP_K = 4
SWIGLU_LIMIT = 7.0
SWIGLU_ALPHA = 1.702
DN_ALPHA = float((2 * DEPTH) ** 0.25)
LN_EPS = 1e-5
RMS_EPS = 1e-6
ROUTER_LANES = 128
ROUTER_PAD = -1e30

ROW_BLK = 256
MM_TM = 1024
MM_TN = 1024
MM_SUB = 256
HG_CHUNK = 128
HG_GROUP = 8
MOE_TM = 512
MOE_TN = 1024
CMB_TM = 128
W_GROUP = 8
VMEM_LIMIT = 56 * 1024 * 1024


def _params(sem):
    return pltpu.CompilerParams(dimension_semantics=sem, vmem_limit_bytes=VMEM_LIMIT)


def _split_bf16(a):
    hi = a.astype(BF16)
    lo = (a - hi.astype(F32)).astype(BF16)
    return hi, lo


def _dot(a, b):
    return jnp.dot(a, b, preferred_element_type=F32)


def _dot3(a, b):
    ah, al = _split_bf16(a)
    bh, bl = _split_bf16(b)
    return _dot(ah, bh) + _dot(ah, bl) + _dot(al, bh)


def _dot_nt(a, b):
    return lax.dot_general(a, b, (((1,), (1,)), ((), ())), preferred_element_type=F32)


def _dot_tn(a, b):
    return lax.dot_general(a, b, (((0,), (0,)), ((), ())), preferred_element_type=F32)


def _sigmoid(x):
    return 1.0 / (1.0 + jnp.exp(-x))


def _ada_kernel(cv_ref, w_ref, b_ref, o_ref):
    x = cv_ref[...]
    s = x * _sigmoid(x)
    o_ref[...] = _dot3(s, w_ref[...]) + b_ref[...]


def _ada(cv, ada_w, ada_b):
    depth, d, n = ada_w.shape
    tn = 1024
    return pl.pallas_call(
        _ada_kernel,
        out_shape=jax.ShapeDtypeStruct((depth, 8, n), F32),
        grid=(depth, n // tn),
        in_specs=[
            pl.BlockSpec((8, d), lambda l, j: (0, 0)),
            pl.BlockSpec((None, d, tn), lambda l, j: (l, 0, j)),
            pl.BlockSpec((None, 1, tn), lambda l, j: (l, 0, j)),
        ],
        out_specs=pl.BlockSpec((None, 8, tn), lambda l, j: (l, 0, j)),
        compiler_params=_params(("arbitrary", "arbitrary")),
        name="ada_mod",
    )(cv, ada_w, ada_b.reshape(depth, 1, n))


def _mod_spec(section, n_x_blocks, ctx_row):
    def index(b, t):
        return (jnp.where(t < n_x_blocks, b, ctx_row), section, 0, 0)
    return pl.BlockSpec((None, None, 1, D_MODEL), index)


def _modulate_kernel(x_ref, sh_ref, sc_ref, o_ref):
    o_ref[...] = (x_ref[...] * (1.0 + sc_ref[...]) + sh_ref[...]).astype(o_ref.dtype)


def _modulate(xcat, mod, n_x_blocks, ctx_row):
    b, l, d = xcat.shape
    blk = pl.BlockSpec((None, ROW_BLK, d), lambda b_, t: (b_, t, 0))
    return pl.pallas_call(
        _modulate_kernel,
        out_shape=jax.ShapeDtypeStruct((b, l, d), BF16),
        grid=(b, l // ROW_BLK),
        in_specs=[blk, _mod_spec(0, n_x_blocks, ctx_row), _mod_spec(1, n_x_blocks, ctx_row)],
        out_specs=blk,
        compiler_params=_params(("parallel", "parallel")),
        name="modulate",
    )(xcat, mod, mod)


def _gelu_tanh(x):
    return 0.5 * x * (1.0 + jnp.tanh(0.7978845608028654 * (x + 0.044715 * (x * x * x))))


def _mm_kernel(a_ref, w_ref, *rest, mode, tiles_per_section):
    o_ref, wbf = rest[-2], rest[-1]

    @pl.when(pl.program_id(1) == 0)
    def _():
        wbf[...] = w_ref[...].astype(BF16)

    sec = pl.program_id(0) // tiles_per_section
    tn = o_ref.shape[1]

    def tile(epilogue):
        for c in range(tn // MM_SUB):
            cs = slice(c * MM_SUB, (c + 1) * MM_SUB)
            o_ref[:, cs] = epilogue(_dot(a_ref[...], wbf[:, cs]), cs)

    def identity(acc, cs):
        return acc

    if mode == "plain":
        tile(identity)
    elif mode == "hgrn_in":
        lb_ref = rest[0]

        @pl.when((sec == 0) | (sec == 4))
        def _():
            tile(lambda acc, cs: acc * _sigmoid(acc))

        @pl.when((sec == 1) | (sec == 2))
        def _():
            tile(lambda acc, cs: lb_ref[:, cs] + (1.0 - lb_ref[:, cs]) * _sigmoid(acc))

        @pl.when(sec == 3)
        def _():
            tile(identity)
    elif mode == "lru_in":
        @pl.when(sec == 0)
        def _():
            tile(lambda acc, cs: _gelu_tanh(acc))

        @pl.when(sec == 1)
        def _():
            tile(identity)
    else:
        raise ValueError(mode)


def _matmul(a, w, mode="plain", lb_row=None, section=None):
    m, k = a.shape
    n = w.shape[1]
    tm = MM_TM if m % MM_TM == 0 else ROW_BLK
    tn = MM_TN
    section = section or n
    in_specs = [
        pl.BlockSpec((tm, k), lambda j, i: (i, 0)),
        pl.BlockSpec((k, tn), lambda j, i: (0, j)),
    ]
    args = [a, w]
    if lb_row is not None:
        in_specs.append(pl.BlockSpec((1, tn), lambda j, i: (0, j)))
        args.append(lb_row)
    return pl.pallas_call(
        functools.partial(_mm_kernel, mode=mode, tiles_per_section=section // tn),
        out_shape=jax.ShapeDtypeStruct((m, n), F32),
        grid=(n // tn, m // tm),
        in_specs=in_specs,
        out_specs=pl.BlockSpec((tm, tn), lambda j, i: (i, j)),
        scratch_shapes=[pltpu.VMEM((k, tn), BF16)],
        compiler_params=_params(("arbitrary", "arbitrary")),
        name="matmul_" + mode,
    )(*args)


def _hgrn_tables(c):
    levels = []
    bs = c
    while bs >= 2:
        levels.append(bs)
        bs //= 2
    nlev = len(levels)
    g = np.zeros((2, (2 + nlev) * c, c), np.float32)
    masks = np.zeros((2, nlev + 1, c, c), np.float32)
    r = np.arange(c)
    for d in range(2):
        for row in range(c):
            if d == 0:
                g[d, row, : row + 1] = 1.0
                g[d, c + row, row + 1:] = 1.0
            else:
                g[d, row, row:] = 1.0
                g[d, c + row, :row] = 1.0
        for li, bs in enumerate(levels):
            half = bs // 2
            for row in range(c):
                base = (row // bs) * bs
                mid = base + half
                pos = row - base
                blk = g[d, (2 + li) * c + row]
                if d == 0:
                    if pos >= half:
                        blk[mid: row + 1] = 1.0
                    else:
                        blk[row + 1: mid] = 1.0
                else:
                    if pos < half:
                        blk[row: mid] = 1.0
                    else:
                        blk[mid: row] = 1.0
            same = (r[:, None] // bs) == (r[None, :] // bs)
            t_hi = (r[:, None] % bs) >= half
            s_hi = (r[None, :] % bs) >= half
            if d == 0:
                masks[d, li] = same & t_hi & ~s_hi
            else:
                masks[d, li] = same & ~t_hi & s_hi
        masks[d, nlev] = np.eye(c)
    return jnp.asarray(g, BF16), jnp.asarray(masks, F32), nlev


def _hgrn_scan_kernel(q_ref, f_ref, v_ref, g_ref, m_ref, o_ref, st_ref, *, nlev):
    c = q_ref.shape[0]

    @pl.when(pl.program_id(3) == 0)
    def _():
        st_ref[...] = jnp.zeros_like(st_ref)

    gmat = g_ref[...]

    for h in range(HG_GROUP):
        sl = slice(h * HG_DK, (h + 1) * HG_DK)
        f = f_ref[:, sl]
        q = q_ref[:, sl]
        vb = v_ref[:, sl].astype(BF16)
        lf = jnp.log(f)
        k = 1.0 - f
        lf_hi, lf_lo = _split_bf16(lf)
        ex2 = _dot(gmat, jnp.concatenate([lf_hi, lf_lo], axis=1))
        ex = ex2[:, :HG_DK] + ex2[:, HG_DK:]
        e = jnp.exp(ex)
        st = st_ref[h]
        o = _dot_nt((q * e[0:c]).astype(BF16), st.astype(BF16))
        scores = m_ref[nlev] * _dot_nt(q.astype(BF16), k.astype(BF16))
        for li in range(nlev):
            el = e[(2 + li) * c:(3 + li) * c]
            scores = scores + m_ref[li] * _dot_nt((q * el).astype(BF16), (k * el).astype(BF16))
        o = o + _dot(scores.astype(BF16), vb)
        o_ref[:, sl] = o
        total = ex[0:1] + ex[c:c + 1]
        st_ref[h] = st * jnp.exp(total) + _dot_tn(vb, (k * e[c:2 * c]).astype(BF16))


def _hgrn_scan(z, n_x):
    b, l, _ = z.shape
    c = HG_CHUNK
    nc = l // c
    ncx = n_x // c
    gmat, masks, nlev = _hgrn_tables(c)

    def chunk(d, s):
        fwd = jnp.where(s < nc - ncx, ncx + s, s - (nc - ncx))
        return jnp.where(d == 0, fwd, nc - 1 - s)

    gw = HG_GROUP * HG_DK
    ngrp = D_MODEL // gw

    def spec(section):
        return pl.BlockSpec((None, c, gw), lambda b_, d, g, s: (b_, chunk(d, s), section(d) * ngrp + g))

    return pl.pallas_call(
        functools.partial(_hgrn_scan_kernel, nlev=nlev),
        out_shape=jax.ShapeDtypeStruct((2, b, l, D_MODEL), F32),
        grid=(b, 2, ngrp, nc),
        in_specs=[
            spec(lambda d: 0),
            spec(lambda d: 1 + d),
            spec(lambda d: 3),
            pl.BlockSpec((None,) + gmat.shape[1:], lambda b_, d, g, s: (d, 0, 0)),
            pl.BlockSpec((None,) + masks.shape[1:], lambda b_, d, g, s: (d, 0, 0, 0)),
        ],
        out_specs=pl.BlockSpec((None, None, c, gw), lambda b_, d, g, s: (d, b_, chunk(d, s), g)),
        scratch_shapes=[pltpu.VMEM((HG_GROUP, HG_DK, HG_DK), F32)],
        compiler_params=_params(("parallel", "arbitrary", "arbitrary", "arbitrary")),
        name="hgrn_scan",
    )(z, z, z, gmat, masks)


def _hgrn_prep_kernel(of_ref, ob_ref, g_ref, ng_ref, o_ref):
    ng = ng_ref[...]
    for h in range(HG_HEADS):
        sl = slice(h * HG_DK, (h + 1) * HG_DK)
        o = of_ref[:, sl] + ob_ref[:, sl]
        ms = jnp.mean(o * o, axis=-1, keepdims=True)
        y = o * lax.rsqrt(ms + RMS_EPS) * ng
        o_ref[:, sl] = (y * g_ref[:, sl]).astype(o_ref.dtype)


def _hgrn_prep(o2, z, norm_g):
    _, m, d = o2.shape
    tm = 512 if m % 512 == 0 else ROW_BLK
    return pl.pallas_call(
        _hgrn_prep_kernel,
        out_shape=jax.ShapeDtypeStruct((m, d), BF16),
        grid=(m // tm,),
        in_specs=[
            pl.BlockSpec((None, tm, d), lambda i: (0, i, 0)),
            pl.BlockSpec((None, tm, d), lambda i: (1, i, 0)),
            pl.BlockSpec((tm, d), lambda i: (i, 4)),
            pl.BlockSpec((1, HG_DK), lambda i: (0, 0)),
        ],
        out_specs=pl.BlockSpec((tm, d), lambda i: (i, 0)),
        compiler_params=_params(("parallel",)),
        name="hgrn_prep",
    )(o2, o2, z, norm_g.reshape(1, HG_DK))


def _layer_norm(v, g, b):
    mu = jnp.mean(v, axis=-1, keepdims=True)
    vc = v - mu
    var = jnp.mean(vc * vc, axis=-1, keepdims=True)
    return vc * lax.rsqrt(var + LN_EPS) * g + b


def _router_topk(h, wr, br, ti_ref, tg_ref):
    logits = _dot3(h, wr) + br
    lane = lax.broadcasted_iota(jnp.int32, logits.shape, 1)
    lane_f = lane.astype(F32)
    vals, ids = [], []
    cur = logits
    for _ in range(TOP_K):
        mx = jnp.max(cur, axis=-1, keepdims=True)
        idx = jnp.min(jnp.where(cur == mx, lane_f, float(ROUTER_LANES)), axis=-1, keepdims=True)
        vals.append(mx)
        ids.append(idx)
        cur = jnp.where(lane_f == idx, -jnp.inf, cur)
    ex = [jnp.exp(v - vals[0]) for v in vals]
    den = ex[0] + ex[1] + ex[2] + ex[3]
    ti = jnp.zeros(logits.shape, F32)
    tg = jnp.zeros(logits.shape, F32)
    for kk in range(TOP_K):
        ti = jnp.where(lane == kk, ids[kk], ti)
        tg = jnp.where(lane == kk, ex[kk] / den, tg)
    ti_ref[...] = ti.astype(jnp.int32)
    tg_ref[...] = tg


def _post_kernel(x_ref, y_ref, gate_ref, lng_ref, lnb_ref, sh_ref, sc_ref, wr_ref, br_ref,
                 xo_ref, h_ref, ti_ref, tg_ref):
    v = DN_ALPHA * x_ref[...] + gate_ref[...] * y_ref[...]
    xn = _layer_norm(v, lng_ref[...], lnb_ref[...])
    xo_ref[...] = xn
    h = xn * (1.0 + sc_ref[...]) + sh_ref[...]
    h_ref[...] = h
    _router_topk(h, wr_ref[...], br_ref[...], ti_ref, tg_ref)


def _post(x, y, mod, ln_g, ln_b, w_router, b_router, n_blocks, n_x_blocks, ctx_row):
    b, _, d = x.shape
    rows = n_blocks * ROW_BLK
    blk = pl.BlockSpec((None, ROW_BLK, d), lambda b_, t: (b_, t, 0))
    lane_blk = pl.BlockSpec((None, ROW_BLK, 128), lambda b_, t: (b_, t, 0))
    row = pl.BlockSpec((1, d), lambda b_, t: (0, 0))
    ms = functools.partial(_mod_spec, n_x_blocks=n_x_blocks, ctx_row=ctx_row)
    wr = jnp.zeros((d, ROUTER_LANES), F32).at[:, :N_EXPERTS].set(w_router)
    br = jnp.full((1, ROUTER_LANES), ROUTER_PAD, F32).at[0, :N_EXPERTS].set(b_router)
    return pl.pallas_call(
        _post_kernel,
        out_shape=(
            jax.ShapeDtypeStruct((b, rows, d), F32),
            jax.ShapeDtypeStruct((b, rows, d), F32),
            jax.ShapeDtypeStruct((b, rows, 128), jnp.int32),
            jax.ShapeDtypeStruct((b, rows, 128), F32),
        ),
        grid=(b, n_blocks),
        in_specs=[blk, blk, ms(2), row, row, ms(3), ms(4),
                  pl.BlockSpec((d, ROUTER_LANES), lambda b_, t: (0, 0)),
                  pl.BlockSpec((1, ROUTER_LANES), lambda b_, t: (0, 0))],
        out_specs=(blk, blk, lane_blk, lane_blk),
        compiler_params=_params(("parallel", "parallel")),
        name="post_mixer",
    )(x, y, mod, ln_g.reshape(1, d), ln_b.reshape(1, d), mod, mod, wr, br)


def _gather_kernel(nu_ref, idx_ref, h_hbm, o_ref, buf, sem):
    i = pl.program_id(0)
    tm = buf.shape[1]
    nu = nu_ref[0]

    def issue(block, slot):
        base = block * tm

        def start(r, carry):
            tok = idx_ref[base + r]
            pltpu.make_async_copy(
                h_hbm.at[pl.ds(tok, 1), :], buf.at[slot, pl.ds(r, 1), :], sem.at[slot]).start()
            return carry

        lax.fori_loop(0, tm, start, 0, unroll=8)

    @pl.when(i == 0)
    def _():
        issue(0, 0)

    @pl.when(i + 1 < nu)
    def _():
        issue(i + 1, (i + 1) % 2)

    @pl.when(i < nu)
    def _():
        slot = i % 2
        pltpu.make_async_copy(buf.at[slot], buf.at[slot], sem.at[slot]).wait()
        o_ref[...] = buf[slot].astype(o_ref.dtype)

    @pl.when(i >= nu)
    def _():
        o_ref[...] = jnp.zeros_like(o_ref)


def _moe_gather(h, row_tok, n_used, n_blocks):
    t, d = h.shape
    tm = MOE_TM
    return pl.pallas_call(
        _gather_kernel,
        out_shape=jax.ShapeDtypeStruct((n_blocks * tm, d), BF16),
        grid_spec=pltpu.PrefetchScalarGridSpec(
            num_scalar_prefetch=2,
            grid=(n_blocks,),
            in_specs=[pl.BlockSpec(memory_space=pl.ANY)],
            out_specs=pl.BlockSpec((tm, d), lambda i, nu, idx: (i, 0)),
            scratch_shapes=[pltpu.VMEM((2, tm, d), F32), pltpu.SemaphoreType.DMA((2,))],
        ),
        compiler_params=_params(("arbitrary",)),
        name="moe_gather",
    )(n_used, row_tok, h)


def _expert_changed(be_ref, i):
    return (i == 0) | (be_ref[i] != be_ref[jnp.maximum(i - 1, 0)])


def _moe_gu_kernel(be_ref, nu_ref, x_ref, w_ref, b_ref, perm_ref, o_ref, wbf):
    i = pl.program_id(1)
    tn = w_ref.shape[1]
    sub = perm_ref.shape[0]
    half = sub // 2

    @pl.when(i < nu_ref[0])
    def _():
        @pl.when(_expert_changed(be_ref, i))
        def _():
            for c in range(tn // sub):
                cs = slice(c * sub, (c + 1) * sub)
                wbf[:, cs] = _dot(w_ref[:, cs].astype(BF16), perm_ref[...]).astype(BF16)

        for c in range(tn // sub):
            cs = slice(c * sub, (c + 1) * sub)
            gu = _dot(x_ref[...], wbf[:, cs]) + b_ref[:, cs]
            g = jnp.minimum(gu[:, :half], SWIGLU_LIMIT)
            u = jnp.clip(gu[:, half:], -SWIGLU_LIMIT, SWIGLU_LIMIT)
            act = (u + 1.0) * (g * _sigmoid(SWIGLU_ALPHA * g))
            o_ref[:, c * half:(c + 1) * half] = act.astype(o_ref.dtype)

    @pl.when(i >= nu_ref[0])
    def _():
        o_ref[...] = jnp.zeros_like(o_ref)


def _moe_down_kernel(be_ref, nu_ref, a_ref, w_ref, b_ref, o_ref, wbf):
    i = pl.program_id(1)

    @pl.when(i < nu_ref[0])
    def _():
        @pl.when(_expert_changed(be_ref, i))
        def _():
            wbf[...] = w_ref[...].astype(BF16)

        o_ref[...] = _dot(a_ref[...], wbf[...]) + b_ref[...]

    @pl.when(i >= nu_ref[0])
    def _():
        o_ref[...] = jnp.zeros_like(o_ref)


def _moe_experts(xs, block_e, n_used, layer, w_gu, b_gu, w_down, b_down):
    n_rows, d = xs.shape
    tm, tn = MOE_TM, MOE_TN
    n_blocks = n_rows // tm
    de2 = w_gu.shape[-1]
    de = de2 // 2
    sub = MM_SUB
    half = sub // 2
    perm = np.zeros((sub, sub), np.float32)
    perm[2 * np.arange(half), np.arange(half)] = 1.0
    perm[2 * np.arange(half) + 1, half + np.arange(half)] = 1.0
    perm = jnp.asarray(perm, BF16)
    b_perm = b_gu.reshape(DEPTH, N_EXPERTS, de2 // sub, half, 2)
    b_perm = jnp.swapaxes(b_perm, -1, -2).reshape(DEPTH, N_EXPERTS, 1, de2)

    def row(i, nu):
        return jnp.minimum(i, nu[0] - 1)

    act = pl.pallas_call(
        _moe_gu_kernel,
        out_shape=jax.ShapeDtypeStruct((n_rows, de), BF16),
        grid_spec=pltpu.PrefetchScalarGridSpec(
            num_scalar_prefetch=2,
            grid=(de2 // tn, n_blocks),
            in_specs=[
                pl.BlockSpec((tm, d), lambda j, i, be, nu: (row(i, nu), 0)),
                pl.BlockSpec((None, None, d, tn), lambda j, i, be, nu: (layer, be[i], 0, j)),
                pl.BlockSpec((None, None, 1, tn), lambda j, i, be, nu: (layer, be[i], 0, j)),
                pl.BlockSpec((sub, sub), lambda j, i, be, nu: (0, 0)),
            ],
            out_specs=pl.BlockSpec((tm, tn // 2), lambda j, i, be, nu: (i, j)),
            scratch_shapes=[pltpu.VMEM((d, tn), BF16)],
        ),
        compiler_params=_params(("arbitrary", "arbitrary")),
        name="moe_gate_up",
    )(block_e, n_used, xs, w_gu, b_perm, perm)

    yb = pl.pallas_call(
        _moe_down_kernel,
        out_shape=jax.ShapeDtypeStruct((n_rows, d), F32),
        grid_spec=pltpu.PrefetchScalarGridSpec(
            num_scalar_prefetch=2,
            grid=(d // tn, n_blocks),
            in_specs=[
                pl.BlockSpec((tm, de), lambda j, i, be, nu: (row(i, nu), 0)),
                pl.BlockSpec((None, None, de, tn), lambda j, i, be, nu: (layer, be[i], 0, j)),
                pl.BlockSpec((None, None, 1, tn), lambda j, i, be, nu: (layer, be[i], 0, j)),
            ],
            out_specs=pl.BlockSpec((tm, tn), lambda j, i, be, nu: (i, j)),
            scratch_shapes=[pltpu.VMEM((de, tn), BF16)],
        ),
        compiler_params=_params(("arbitrary", "arbitrary")),
        name="moe_down",
    )(block_e, n_used, act, w_down, b_down.reshape(DEPTH, N_EXPERTS, 1, d))
    return yb


def _combine_kernel(dest_ref, tg_ref, x_ref, gate_ref, lng_ref, lnb_ref, *rest, with_next):
    if with_next:
        sh_ref, sc_ref, yb_hbm, xo_ref, h_ref, buf, sem = rest
    else:
        yb_hbm, xo_ref, buf, sem = rest
    tmc = buf.shape[2]
    step = pl.program_id(0) * pl.num_programs(1) + pl.program_id(1)
    n_steps = pl.num_programs(0) * pl.num_programs(1)

    def issue(stp, slot):
        base = stp * (tmc * TOP_K)

        def start(t, carry):
            for kk in range(TOP_K):
                row = dest_ref[base + t * TOP_K + kk]
                pltpu.make_async_copy(
                    yb_hbm.at[pl.ds(row, 1), :], buf.at[slot, kk, pl.ds(t, 1), :], sem.at[slot]).start()
            return carry

        lax.fori_loop(0, tmc, start, 0, unroll=4)

    @pl.when(step == 0)
    def _():
        issue(0, 0)

    @pl.when(step + 1 < n_steps)
    def _():
        issue(step + 1, (step + 1) % 2)

    slot = step % 2
    pltpu.make_async_copy(buf.at[slot], buf.at[slot], sem.at[slot]).wait()

    tg = tg_ref[...]
    y = tg[:, 0:1] * buf[slot, 0]
    for kk in range(1, TOP_K):
        y = y + tg[:, kk:kk + 1] * buf[slot, kk]
    v = DN_ALPHA * x_ref[...] + gate_ref[...] * y
    xn = _layer_norm(v, lng_ref[...], lnb_ref[...])
    xo_ref[...] = xn
    if with_next:
        h_ref[...] = (xn * (1.0 + sc_ref[...]) + sh_ref[...]).astype(h_ref.dtype)


def _combine(x, yb, dest, tg, mod, ln_g, ln_b, n_x_blocks, ctx_row, mod_next=None):
    b, l, d = x.shape
    tmc = CMB_TM
    per = ROW_BLK // tmc
    with_next = mod_next is not None

    def mod_spec(section):
        def index(b_, t, dest_):
            return (jnp.where(t // per < n_x_blocks, b_, ctx_row), section, 0, 0)
        return pl.BlockSpec((None, None, 1, d), index)

    blk = pl.BlockSpec((None, tmc, d), lambda b_, t, dest_: (b_, t, 0))
    row = pl.BlockSpec((1, d), lambda b_, t, dest_: (0, 0))
    in_specs = [pl.BlockSpec((None, tmc, 128), lambda b_, t, dest_: (b_, t, 0)), blk, mod_spec(5), row, row]
    args = [tg, x, mod, ln_g.reshape(1, d), ln_b.reshape(1, d)]
    out_shape = [jax.ShapeDtypeStruct((b, l, d), F32)]
    out_specs = [blk]
    if with_next:
        in_specs += [mod_spec(0), mod_spec(1)]
        args += [mod_next, mod_next]
        out_shape.append(jax.ShapeDtypeStruct((b, l, d), BF16))
        out_specs.append(blk)
    in_specs.append(pl.BlockSpec(memory_space=pl.ANY))
    args.append(yb)
    return pl.pallas_call(
        functools.partial(_combine_kernel, with_next=with_next),
        out_shape=tuple(out_shape),
        grid_spec=pltpu.PrefetchScalarGridSpec(
            num_scalar_prefetch=1,
            grid=(b, l // tmc),
            in_specs=in_specs,
            out_specs=tuple(out_specs),
            scratch_shapes=[pltpu.VMEM((2, TOP_K, tmc, d), F32), pltpu.SemaphoreType.DMA((2,))],
        ),
        compiler_params=_params(("arbitrary", "arbitrary")),
        name="moe_combine",
    )(dest, *args)


def _moe_plan(top_i):
    t = top_i.shape[0]
    n_pairs = t * TOP_K
    tm = MOE_TM
    e_flat = top_i.reshape(-1)
    onehot = (e_flat[:, None] == jnp.arange(N_EXPERTS, dtype=jnp.int32)[None, :]).astype(jnp.int32)
    csum = jnp.cumsum(onehot, axis=0)
    rank = jnp.take_along_axis(csum, e_flat[:, None], axis=1)[:, 0] - 1
    counts = csum[-1]
    padded = (counts + tm - 1) // tm * tm
    pend = jnp.cumsum(padded)
    pstart = pend - padded
    dest = (pstart[e_flat] + rank).astype(jnp.int32)
    n_blocks = (n_pairs + N_EXPERTS * (tm - 1) + tm - 1) // tm
    n_rows = n_blocks * tm
    tok = jnp.arange(n_pairs, dtype=jnp.int32) // TOP_K
    row_tok = jnp.zeros((n_rows,), jnp.int32).at[dest].set(tok)
    n_used = (pend[-1] // tm).astype(jnp.int32)
    blk = jnp.arange(n_blocks, dtype=jnp.int32)
    block_e = jnp.searchsorted(pend, jnp.minimum(blk, n_used - 1) * tm, side="right")
    block_e = jnp.minimum(block_e, N_EXPERTS - 1).astype(jnp.int32)
    return dest, row_tok, block_e, n_used.reshape(1), n_blocks


def _moe(h, top_i, layer, w_gu, b_gu, w_down, b_down):
    dest, row_tok, block_e, n_used, n_blocks = _moe_plan(top_i)
    xs = _moe_gather(h, row_tok, n_used, n_blocks)
    yb = _moe_experts(xs, block_e, n_used, layer, w_gu, b_gu, w_down, b_down)
    return yb, dest


def _conv_ctx_kernel(u_ref, w_ref, b_ref, o_ref):
    u = u_ref[...]
    n = u.shape[0]
    row = lax.broadcasted_iota(jnp.int32, u.shape, 0)
    w = w_ref[...]
    um2 = jnp.where(row >= 2, pltpu.roll(u, 2, 0), 0.0)
    um1 = jnp.where(row >= 1, pltpu.roll(u, 1, 0), 0.0)
    up1 = jnp.where(row < n - 1, pltpu.roll(u, n - 1, 0), 0.0)
    o_ref[...] = w[0:1] * um2 + w[1:2] * um1 + w[2:3] * u + w[3:4] * up1 + b_ref[...]


def _conv_ctx(z1, conv_w, conv_b, n_x, n_ctx):
    b, l, _ = z1.shape
    w = conv_w.shape[1]
    return pl.pallas_call(
        _conv_ctx_kernel,
        out_shape=jax.ShapeDtypeStruct((b, n_ctx, w), F32),
        grid=(b,),
        in_specs=[
            pl.BlockSpec((None, n_ctx, w), lambda b_: (b_, n_x // n_ctx, 1)),
            pl.BlockSpec((CONV_W, w), lambda b_: (0, 0)),
            pl.BlockSpec((1, w), lambda b_: (0, 0)),
        ],
        out_specs=pl.BlockSpec((None, n_ctx, w), lambda b_: (b_, 0, 0)),
        compiler_params=_params(("parallel",)),
        name="lru_conv_ctx",
    )(z1, conv_w, conv_b.reshape(1, w))


def _conv_x_kernel(u_ref, prev_ref, next_ref, w_ref, b_ref, o_ref):
    g = pl.program_id(1)
    ng = pl.num_programs(1)
    r = u_ref.shape[0]
    wg = u_ref.shape[1]
    w = w_ref[...]
    c0, c1, c2, c3 = w[0:1], w[1:2], w[2:3], w[3:4]
    bias = b_ref[...]
    sub = lax.broadcasted_iota(jnp.int32, u_ref.shape[1:], 0)

    def from_prev_col(x, halo):
        return jnp.where(sub == 0, halo, pltpu.roll(x, 1, 0))

    def from_next_col(x, halo):
        return jnp.where(sub == wg - 1, halo, pltpu.roll(x, wg - 1, 0))

    has_prev = (g > 0).astype(F32)
    has_next = (g < ng - 1).astype(F32)
    hp = prev_ref.shape[0]
    u_m1 = from_prev_col(u_ref[r - 1], prev_ref[hp - 1, wg - 1:wg, :] * has_prev)
    u_m2 = from_prev_col(u_ref[r - 2], prev_ref[hp - 2, wg - 1:wg, :] * has_prev)
    u_p = from_next_col(u_ref[0], next_ref[0, 0:1, :] * has_next)

    def interior(i, carry):
        o_ref[i] = c0 * u_ref[i - 2] + c1 * u_ref[i - 1] + c2 * u_ref[i] + c3 * u_ref[i + 1] + bias
        return carry

    lax.fori_loop(2, r - 1, interior, 0)
    o_ref[0] = c0 * u_m2 + c1 * u_m1 + c2 * u_ref[0] + c3 * u_ref[1] + bias
    o_ref[1] = c0 * u_m1 + c1 * u_ref[0] + c2 * u_ref[1] + c3 * u_ref[2] + bias
    o_ref[r - 1] = c0 * u_ref[r - 3] + c1 * u_ref[r - 2] + c2 * u_ref[r - 1] + c3 * u_p + bias


def _conv_x(z1, conv_w, conv_b, n_x):
    b, l, f = z1.shape
    w = conv_w.shape[1]
    rows = n_x // GRID_W
    wg = W_GROUP
    ng = GRID_W // wg
    z4 = z1.reshape(b, l // GRID_W, GRID_W, f)
    halo = 8
    last_halo = rows // halo - 1
    return pl.pallas_call(
        _conv_x_kernel,
        out_shape=jax.ShapeDtypeStruct((b, rows, GRID_W, w), F32),
        grid=(b, ng),
        in_specs=[
            pl.BlockSpec((None, rows, wg, w), lambda b_, g: (b_, 0, g, 1)),
            pl.BlockSpec((None, halo, wg, w), lambda b_, g: (b_, last_halo, jnp.maximum(g - 1, 0), 1)),
            pl.BlockSpec((None, halo, wg, w), lambda b_, g: (b_, 0, jnp.minimum(g + 1, ng - 1), 1)),
            pl.BlockSpec((CONV_W, w), lambda b_, g: (0, 0)),
            pl.BlockSpec((1, w), lambda b_, g: (0, 0)),
        ],
        out_specs=pl.BlockSpec((None, rows, wg, w), lambda b_, g: (b_, 0, g, 0)),
        compiler_params=_params(("parallel", "parallel")),
        name="lru_conv_x",
    )(z4, z4, z4, conv_w, conv_b.reshape(1, w))


def _gates_kernel(u_ref, wa_ref, wx_ref, ba_ref, bx_ref, lam_ref, a_ref, b_ref, wabf, wxbf):
    @pl.when(pl.program_id(0) == 0)
    def _():
        wabf[...] = wa_ref[...].astype(BF16)
        wxbf[...] = wx_ref[...].astype(BF16)

    for d in range(2):
        lam = lam_ref[d]
        nl = -lam
        softplus = jnp.maximum(nl, 0.0) + jnp.log(1.0 + jnp.exp(-jnp.abs(nl)))
        for h in range(LRU_HEADS):
            sl = slice(h * LRU_BLOCK, (h + 1) * LRU_BLOCK)
            u = u_ref[:, sl]
            ub = u.astype(BF16)
            r = _sigmoid(_dot(ub, wabf[d, h]) + ba_ref[d][:, sl])
            ig = _sigmoid(_dot(ub, wxbf[d, h]) + bx_ref[d][:, sl])
            log_a = (-LRU_C) * r * softplus[:, sl]
            a = jnp.exp(log_a)
            a_ref[d, :, sl] = a
            b_ref[d, :, sl] = jnp.sqrt(1.0 - a * a) * (ig * u)


def _gates(cu, w_a, w_x, b_a, b_x, lam):
    m, w = cu.shape
    tm = ROW_BLK
    full5 = pl.BlockSpec(w_a.shape, lambda i: (0, 0, 0, 0))
    vec = pl.BlockSpec((2, 1, w), lambda i: (0, 0, 0))
    out = pl.BlockSpec((2, tm, w), lambda i: (0, i, 0))
    return pl.pallas_call(
        _gates_kernel,
        out_shape=(jax.ShapeDtypeStruct((2, m, w), F32), jax.ShapeDtypeStruct((2, m, w), F32)),
        grid=(m // tm,),
        in_specs=[pl.BlockSpec((tm, w), lambda i: (i, 0)), full5, full5, vec, vec, vec],
        out_specs=(out, out),
        scratch_shapes=[pltpu.VMEM(w_a.shape, BF16), pltpu.VMEM(w_a.shape, BF16)],
        compiler_params=_params(("arbitrary",)),
        name="lru_gates",
    )(cu, w_a, w_x, b_a.reshape(2, 1, w), b_x.reshape(2, 1, w), lam.reshape(2, 1, w))


def _lru_scan_kernel(ax_ref, bx_ref, ac_ref, bc_ref, h_ref, carry, acum, *, rev):
    s = pl.program_id(2)
    rows = ax_ref.shape[0]
    wg = ax_ref.shape[1]
    n_ctx = ac_ref.shape[0]
    ft = ax_ref.shape[2]

    @pl.when(s == 0)
    def _():
        def body(t, h):
            tt = n_ctx - 1 - t if rev else t
            return ac_ref[pl.ds(tt, 1), :] * h + bc_ref[pl.ds(tt, 1), :]

        carry[...] = lax.fori_loop(0, n_ctx, body, jnp.zeros((1, ft), F32))

    @pl.when(s > 0)
    def _():
        def seg(i, c):
            a_run, h_run = c
            r = rows - 1 - i if rev else i
            a = ax_ref[r]
            h_run = a * h_run + bx_ref[r]
            a_run = a_run * a
            h_ref[r] = h_run
            acum[r] = a_run
            return a_run, h_run

        a_tot, h_tot = lax.fori_loop(
            0, rows, seg, (jnp.ones((wg, ft), F32), jnp.zeros((wg, ft), F32)))
        h = carry[...]
        sub = lax.broadcasted_iota(jnp.int32, (wg, ft), 0)
        h_in = jnp.zeros((wg, ft), F32)
        for j in (range(wg - 1, -1, -1) if rev else range(wg)):
            h_in = jnp.where(sub == j, h, h_in)
            h = a_tot[j:j + 1] * h + h_tot[j:j + 1]
        carry[...] = h

        def fix(r, c):
            h_ref[r] = h_ref[r] + acum[r] * h_in
            return c

        lax.fori_loop(0, rows, fix, 0)


def _lru_scan(a_x, b_x, a_c, b_c, d, rev):
    _, b, rows, gw, w = a_x.shape
    n_ctx = a_c.shape[2]
    wg = W_GROUP
    ng = gw // wg
    ft = 1024

    def grp(s):
        k = jnp.maximum(s - 1, 0)
        return ng - 1 - k if rev else k

    xblk = pl.BlockSpec((None, None, rows, wg, ft), lambda b_, f, s: (d, b_, 0, grp(s), f))
    cblk = pl.BlockSpec((None, None, n_ctx, ft), lambda b_, f, s: (d, b_, 0, f))
    return pl.pallas_call(
        functools.partial(_lru_scan_kernel, rev=rev),
        out_shape=jax.ShapeDtypeStruct((b, rows, gw, w), F32),
        grid=(b, w // ft, ng + 1),
        in_specs=[xblk, xblk, cblk, cblk],
        out_specs=pl.BlockSpec((None, rows, wg, ft), lambda b_, f, s: (b_, 0, grp(s), f)),
        scratch_shapes=[pltpu.VMEM((1, ft), F32), pltpu.VMEM((rows, wg, ft), F32)],
        compiler_params=_params(("parallel", "parallel", "arbitrary")),
        name="lru_scan_bwd" if rev else "lru_scan_fwd",
    )(a_x, b_x, a_c, b_c)


def _lru_prep_kernel(y_ref, hf_ref, hb_ref, o_ref):
    o_ref[...] = (y_ref[...] * (hf_ref[...] + hb_ref[...])).astype(o_ref.dtype)


def _lru_prep(z1, h_f, h_b, n_x):
    b, l, _ = z1.shape
    w = h_f.shape[-1]
    blk = pl.BlockSpec((None, ROW_BLK, w), lambda b_, t: (b_, t, 0))
    return pl.pallas_call(
        _lru_prep_kernel,
        out_shape=jax.ShapeDtypeStruct((b, n_x, w), BF16),
        grid=(b, n_x // ROW_BLK),
        in_specs=[blk, blk, blk],
        out_specs=blk,
        compiler_params=_params(("parallel", "parallel")),
        name="lru_prep",
    )(z1, h_f, h_b)


def kernel(x, c, ctx, c_ctx, ada_w, ada_b, ln1_g, ln1_b, ln2_g, ln2_b, hg_w_in, hg_lb_logits, hg_norm_g, hg_w_out, lru_w_in, lru_conv_w, lru_conv_b, lru_w_a, lru_b_a, lru_w_x, lru_b_x, lru_lam, lru_w_out, moe_w_router, moe_b_router, moe_w_gu, moe_b_gu, moe_w_down, moe_b_down):
    bsz, n_x, d = x.shape
    n_ctx = ctx.shape[1]
    assert d == D_MODEL and ada_w.shape[0] == DEPTH == 2
    assert n_x % ROW_BLK == 0 and n_ctx == ROW_BLK and n_x % (GRID_W * 8) == 0 and bsz < 8
    l = n_x + n_ctx
    n_x_blocks = n_x // ROW_BLK
    n_blocks = l // ROW_BLK
    ctx_row = bsz

    cv = jnp.zeros((8, d), F32).at[:bsz].set(c).at[bsz].set(c_ctx)
    mod = _ada(cv, ada_w, ada_b).reshape(DEPTH, 8, 6, 1, d)
    xcat = jnp.concatenate([x, ctx], axis=1)

    h0 = _modulate(xcat, mod[0], n_x_blocks, ctx_row)
    lb = jnp.cumsum(jax.nn.softmax(hg_lb_logits.astype(F32), axis=1), axis=1)[:, 0]
    lb_row = jnp.zeros((1, 5 * d), F32).at[0, d:3 * d].set(lb.reshape(-1))
    z = _matmul(h0.reshape(bsz * l, d), hg_w_in[0], mode="hgrn_in", lb_row=lb_row, section=d)
    o2 = _hgrn_scan(z.reshape(bsz, l, 5 * d), n_x)
    a0 = _hgrn_prep(o2.reshape(2, bsz * l, d), z, hg_norm_g[0])
    y0 = _matmul(a0, hg_w_out[0]).reshape(bsz, l, d)
    x1, h1, ti, tg = _post(xcat, y0, mod[0], ln1_g[0], ln1_b[0], moe_w_router[0], moe_b_router[0],
                           n_blocks, n_x_blocks, ctx_row)
    yb, dest = _moe(h1.reshape(bsz * l, d), ti.reshape(bsz * l, 128)[:, :TOP_K], 0,
                    moe_w_gu, moe_b_gu, moe_w_down, moe_b_down)
    x2, h2 = _combine(x1, yb, dest, tg, mod[0], ln2_g[0], ln2_b[0], n_x_blocks, ctx_row, mod_next=mod[1])

    z1 = _matmul(h2.reshape(bsz * l, d), lru_w_in[0], mode="lru_in", section=d).reshape(bsz, l, 2 * d)
    cu_x = _conv_x(z1, lru_conv_w[0], lru_conv_b[0], n_x)
    cu_c = _conv_ctx(z1, lru_conv_w[0], lru_conv_b[0], n_x, n_ctx)
    rows = n_x // GRID_W
    a_x, b_x = _gates(cu_x.reshape(bsz * n_x, d), lru_w_a[0], lru_w_x[0], lru_b_a[0], lru_b_x[0], lru_lam[0])
    a_c, b_c = _gates(cu_c.reshape(bsz * n_ctx, d), lru_w_a[0], lru_w_x[0], lru_b_a[0], lru_b_x[0], lru_lam[0])
    a_x = a_x.reshape(2, bsz, rows, GRID_W, d)
    b_x = b_x.reshape(2, bsz, rows, GRID_W, d)
    a_c = a_c.reshape(2, bsz, n_ctx, d)
    b_c = b_c.reshape(2, bsz, n_ctx, d)
    h_f = _lru_scan(a_x, b_x, a_c, b_c, 0, rev=False).reshape(bsz, n_x, d)
    h_b = _lru_scan(a_x, b_x, a_c, b_c, 1, rev=True).reshape(bsz, n_x, d)
    a1 = _lru_prep(z1, h_f, h_b, n_x)
    y1 = _matmul(a1.reshape(bsz * n_x, d), lru_w_out[0]).reshape(bsz, n_x, d)
    x3, h3, ti, tg = _post(x2, y1, mod[1], ln1_g[1], ln1_b[1], moe_w_router[1], moe_b_router[1],
                           n_x_blocks, n_x_blocks, ctx_row)
    yb, dest = _moe(h3.reshape(bsz * n_x, d), ti.reshape(bsz * n_x, 128)[:, :TOP_K], 1,
                    moe_w_gu, moe_b_gu, moe_w_down, moe_b_down)
    (out,) = _combine(x3, yb, dest, tg, mod[1], ln2_g[1], ln2_b[1], n_x_blocks, ctx_row)
    return out
```

```python
import functools

import numpy as np
import jax
import jax.numpy as jnp
from jax import lax
from jax.experimental import pallas as pl
from jax.experimental.pallas import tpu as pltpu

F32 = jnp.float32
BF16 = jnp.bfloat16

D_MODEL = 2048
DEPTH = 2
GRID_W = 64
HG_HEADS = 16
HG_DK = 128
LRU_HEADS = 8
LRU_BLOCK = D_MODEL // LRU_HEADS
CONV_W = 4
LRU_C = 8.0
N_EXPERTS = 32
TOP_K = 4
SWIGLU_LIMIT = 7.0
SWIGLU_ALPHA = 1.702
DN_ALPHA = float((2 * DEPTH) ** 0.25)
LN_EPS = 1e-5
RMS_EPS = 1e-6
ROUTER_LANES = 128
ROUTER_PAD = -1e30

ROW_BLK = 256
MM_TM = 1024
MM_TN = 1024
MM_SUB = 256
HG_CHUNK = 128
HG_GROUP = 8
MOE_TM = 512
MOE_TN = 1024
CMB_TM = 128
W_GROUP = 8
VMEM_LIMIT = 56 * 1024 * 1024


def _params(sem):
    return pltpu.CompilerParams(dimension_semantics=sem, vmem_limit_bytes=VMEM_LIMIT)


def _split_bf16(a):
    hi = a.astype(BF16)
    lo = (a - hi.astype(F32)).astype(BF16)
    return hi, lo


def _dot(a, b):
    return jnp.dot(a, b, preferred_element_type=F32)


def _dot3(a, b):
    ah, al = _split_bf16(a)
    bh, bl = _split_bf16(b)
    return _dot(ah, bh) + _dot(ah, bl) + _dot(al, bh)


def _dot_nt(a, b):
    return lax.dot_general(a, b, (((1,), (1,)), ((), ())), preferred_element_type=F32)


def _dot_tn(a, b):
    return lax.dot_general(a, b, (((0,), (0,)), ((), ())), preferred_element_type=F32)


def _sigmoid(x):
    return 1.0 / (1.0 + jnp.exp(-x))


def _ada_kernel(cv_ref, w_ref, b_ref, o_ref):
    x = cv_ref[...]
    s = x * _sigmoid(x)
    o_ref[...] = _dot3(s, w_ref[...]) + b_ref[...]


def _ada(cv, ada_w, ada_b):
    depth, d, n = ada_w.shape
    tn = 1024
    return pl.pallas_call(
        _ada_kernel,
        out_shape=jax.ShapeDtypeStruct((depth, 8, n), F32),
        grid=(depth, n // tn),
        in_specs=[
            pl.BlockSpec((8, d), lambda l, j: (0, 0)),
            pl.BlockSpec((None, d, tn), lambda l, j: (l, 0, j)),
            pl.BlockSpec((None, 1, tn), lambda l, j: (l, 0, j)),
        ],
        out_specs=pl.BlockSpec((None, 8, tn), lambda l, j: (l, 0, j)),
        compiler_params=_params(("arbitrary", "arbitrary")),
        name="ada_mod",
    )(cv, ada_w, ada_b.reshape(depth, 1, n))


def _mod_spec(section, n_x_blocks, ctx_row):
    def index(b, t):
        return (jnp.where(t < n_x_blocks, b, ctx_row), section, 0, 0)
    return pl.BlockSpec((None, None, 1, D_MODEL), index)


def _modulate_kernel(x_ref, sh_ref, sc_ref, o_ref):
    o_ref[...] = (x_ref[...] * (1.0 + sc_ref[...]) + sh_ref[...]).astype(o_ref.dtype)


def _modulate(xcat, mod, n_x_blocks, ctx_row):
    b, l, d = xcat.shape
    blk = pl.BlockSpec((None, ROW_BLK, d), lambda b_, t: (b_, t, 0))
    return pl.pallas_call(
        _modulate_kernel,
        out_shape=jax.ShapeDtypeStruct((b, l, d), BF16),
        grid=(b, l // ROW_BLK),
        in_specs=[blk, _mod_spec(0, n_x_blocks, ctx_row), _mod_spec(1, n_x_blocks, ctx_row)],
        out_specs=blk,
        compiler_params=_params(("parallel", "parallel")),
        name="modulate",
    )(xcat, mod, mod)


def _gelu_tanh(x):
    return 0.5 * x * (1.0 + jnp.tanh(0.7978845608028654 * (x + 0.044715 * (x * x * x))))


def _mm_kernel(a_ref, w_ref, *rest, mode, tiles_per_section):
    o_ref, wbf = rest[-2], rest[-1]

    @pl.when(pl.program_id(1) == 0)
    def _():
        wbf[...] = w_ref[...].astype(BF16)

    sec = pl.program_id(0) // tiles_per_section
    tn = o_ref.shape[1]

    def tile(epilogue):
        for c in range(tn // MM_SUB):
            cs = slice(c * MM_SUB, (c + 1) * MM_SUB)
            o_ref[:, cs] = epilogue(_dot(a_ref[...], wbf[:, cs]), cs)

    def identity(acc, cs):
        return acc

    if mode == "plain":
        tile(identity)
    elif mode == "hgrn_in":
        lb_ref = rest[0]

        @pl.when((sec == 0) | (sec == 4))
        def _():
            tile(lambda acc, cs: acc * _sigmoid(acc))

        @pl.when((sec == 1) | (sec == 2))
        def _():
            tile(lambda acc, cs: lb_ref[:, cs] + (1.0 - lb_ref[:, cs]) * _sigmoid(acc))

        @pl.when(sec == 3)
        def _():
            tile(identity)
    elif mode == "lru_in":
        @pl.when(sec == 0)
        def _():
            tile(lambda acc, cs: _gelu_tanh(acc))

        @pl.when(sec == 1)
        def _():
            tile(identity)
    else:
        raise ValueError(mode)


def _matmul(a, w, mode="plain", lb_row=None, section=None):
    m, k = a.shape
    n = w.shape[1]
    tm = MM_TM if m % MM_TM == 0 else ROW_BLK
    tn = MM_TN
    section = section or n
    in_specs = [
        pl.BlockSpec((tm, k), lambda j, i: (i, 0)),
        pl.BlockSpec((k, tn), lambda j, i: (0, j)),
    ]
    args = [a, w]
    if lb_row is not None:
        in_specs.append(pl.BlockSpec((1, tn), lambda j, i: (0, j)))
        args.append(lb_row)
    return pl.pallas_call(
        functools.partial(_mm_kernel, mode=mode, tiles_per_section=section // tn),
        out_shape=jax.ShapeDtypeStruct((m, n), F32),
        grid=(n // tn, m // tm),
        in_specs=in_specs,
        out_specs=pl.BlockSpec((tm, tn), lambda j, i: (i, j)),
        scratch_shapes=[pltpu.VMEM((k, tn), BF16)],
        compiler_params=_params(("arbitrary", "arbitrary")),
        name="matmul_" + mode,
    )(*args)


def _hgrn_tables(c):
    levels = []
    bs = c
    while bs >= 2:
        levels.append(bs)
        bs //= 2
    nlev = len(levels)
    g = np.zeros((2, (2 + nlev) * c, c), np.float32)
    masks = np.zeros((2, nlev + 1, c, c), np.float32)
    r = np.arange(c)
    for d in range(2):
        for row in range(c):
            if d == 0:
                g[d, row, : row + 1] = 1.0
                g[d, c + row, row + 1:] = 1.0
            else:
                g[d, row, row:] = 1.0
                g[d, c + row, :row] = 1.0
        for li, bs in enumerate(levels):
            half = bs // 2
            for row in range(c):
                base = (row // bs) * bs
                mid = base + half
                pos = row - base
                blk = g[d, (2 + li) * c + row]
                if d == 0:
                    if pos >= half:
                        blk[mid: row + 1] = 1.0
                    else:
                        blk[row + 1: mid] = 1.0
                else:
                    if pos < half:
                        blk[row: mid] = 1.0
                    else:
                        blk[mid: row] = 1.0
            same = (r[:, None] // bs) == (r[None, :] // bs)
            t_hi = (r[:, None] % bs) >= half
            s_hi = (r[None, :] % bs) >= half
            if d == 0:
                masks[d, li] = same & t_hi & ~s_hi
            else:
                masks[d, li] = same & ~t_hi & s_hi
        masks[d, nlev] = np.eye(c)
    return jnp.asarray(g, BF16), jnp.asarray(masks, F32), nlev


def _hgrn_scan_kernel(q_ref, f_ref, v_ref, g_ref, m_ref, o_ref, st_ref, *, nlev):
    c = q_ref.shape[0]

    @pl.when(pl.program_id(3) == 0)
    def _():
        st_ref[...] = jnp.zeros_like(st_ref)

    gmat = g_ref[...]

    for h in range(HG_GROUP):
        sl = slice(h * HG_DK, (h + 1) * HG_DK)
        f = f_ref[:, sl]
        q = q_ref[:, sl]
        vb = v_ref[:, sl].astype(BF16)
        lf = jnp.log(f)
        k = 1.0 - f
        lf_hi, lf_lo = _split_bf16(lf)
        ex2 = _dot(gmat, jnp.concatenate([lf_hi, lf_lo], axis=1))
        ex = ex2[:, :HG_DK] + ex2[:, HG_DK:]
        e = jnp.exp(ex)
        st = st_ref[h]
        o = _dot_nt((q * e[0:c]).astype(BF16), st.astype(BF16))
        scores = m_ref[nlev] * _dot_nt(q.astype(BF16), k.astype(BF16))
        for li in range(nlev):
            el = e[(2 + li) * c:(3 + li) * c]
            scores = scores + m_ref[li] * _dot_nt((q * el).astype(BF16), (k * el).astype(BF16))
        o = o + _dot(scores.astype(BF16), vb)
        o_ref[:, sl] = o
        total = ex[0:1] + ex[c:c + 1]
        st_ref[h] = st * jnp.exp(total) + _dot_tn(vb, (k * e[c:2 * c]).astype(BF16))


def _hgrn_scan(z, n_x):
    b, l, _ = z.shape
    c = HG_CHUNK
    nc = l // c
    ncx = n_x // c
    gmat, masks, nlev = _hgrn_tables(c)

    def chunk(d, s):
        fwd = jnp.where(s < nc - ncx, ncx + s, s - (nc - ncx))
        return jnp.where(d == 0, fwd, nc - 1 - s)

    gw = HG_GROUP * HG_DK
    ngrp = D_MODEL // gw

    def spec(section):
        return pl.BlockSpec((None, c, gw), lambda b_, d, g, s: (b_, chunk(d, s), section(d) * ngrp + g))

    return pl.pallas_call(
        functools.partial(_hgrn_scan_kernel, nlev=nlev),
        out_shape=jax.ShapeDtypeStruct((2, b, l, D_MODEL), F32),
        grid=(b, 2, ngrp, nc),
        in_specs=[
            spec(lambda d: 0),
            spec(lambda d: 1 + d),
            spec(lambda d: 3),
            pl.BlockSpec((None,) + gmat.shape[1:], lambda b_, d, g, s: (d, 0, 0)),
            pl.BlockSpec((None,) + masks.shape[1:], lambda b_, d, g, s: (d, 0, 0, 0)),
        ],
        out_specs=pl.BlockSpec((None, None, c, gw), lambda b_, d, g, s: (d, b_, chunk(d, s), g)),
        scratch_shapes=[pltpu.VMEM((HG_GROUP, HG_DK, HG_DK), F32)],
        compiler_params=_params(("parallel", "arbitrary", "arbitrary", "arbitrary")),
        name="hgrn_scan",
    )(z, z, z, gmat, masks)


def _hgrn_prep_kernel(of_ref, ob_ref, g_ref, ng_ref, o_ref):
    ng = ng_ref[...]
    for h in range(HG_HEADS):
        sl = slice(h * HG_DK, (h + 1) * HG_DK)
        o = of_ref[:, sl] + ob_ref[:, sl]
        ms = jnp.mean(o * o, axis=-1, keepdims=True)
        y = o * lax.rsqrt(ms + RMS_EPS) * ng
        o_ref[:, sl] = (y * g_ref[:, sl]).astype(o_ref.dtype)


def _hgrn_prep(o2, z, norm_g):
    _, m, d = o2.shape
    tm = 512 if m % 512 == 0 else ROW_BLK
    return pl.pallas_call(
        _hgrn_prep_kernel,
        out_shape=jax.ShapeDtypeStruct((m, d), BF16),
        grid=(m // tm,),
        in_specs=[
            pl.BlockSpec((None, tm, d), lambda i: (0, i, 0)),
            pl.BlockSpec((None, tm, d), lambda i: (1, i, 0)),
            pl.BlockSpec((tm, d), lambda i: (i, 4)),
            pl.BlockSpec((1, HG_DK), lambda i: (0, 0)),
        ],
        out_specs=pl.BlockSpec((tm, d), lambda i: (i, 0)),
        compiler_params=_params(("parallel",)),
        name="hgrn_prep",
    )(o2, o2, z, norm_g.reshape(1, HG_DK))


def _layer_norm(v, g, b):
    mu = jnp.mean(v, axis=-1, keepdims=True)
    vc = v - mu
    var = jnp.mean(vc * vc, axis=-1, keepdims=True)
    return vc * lax.rsqrt(var + LN_EPS) * g + b


def _router_topk(h, wr, br, ti_ref, tg_ref):
    logits = _dot3(h, wr) + br
    lane = lax.broadcasted_iota(jnp.int32, logits.shape, 1)
    lane_f = lane.astype(F32)
    vals, ids = [], []
    cur = logits
    for _ in range(TOP_K):
        mx = jnp.max(cur, axis=-1, keepdims=True)
        idx = jnp.min(jnp.where(cur == mx, lane_f, float(ROUTER_LANES)), axis=-1, keepdims=True)
        vals.append(mx)
        ids.append(idx)
        cur = jnp.where(lane_f == idx, -jnp.inf, cur)
    ex = [jnp.exp(v - vals[0]) for v in vals]
    den = ex[0] + ex[1] + ex[2] + ex[3]
    ti = jnp.zeros(logits.shape, F32)
    tg = jnp.zeros(logits.shape, F32)
    for kk in range(TOP_K):
        ti = jnp.where(lane == kk, ids[kk], ti)
        tg = jnp.where(lane == kk, ex[kk] / den, tg)
    ti_ref[...] = ti.astype(jnp.int32)
    tg_ref[...] = tg


def _post_kernel(x_ref, y_ref, gate_ref, lng_ref, lnb_ref, sh_ref, sc_ref, wr_ref, br_ref,
                 xo_ref, h_ref, ti_ref, tg_ref):
    v = DN_ALPHA * x_ref[...] + gate_ref[...] * y_ref[...]
    xn = _layer_norm(v, lng_ref[...], lnb_ref[...])
    xo_ref[...] = xn
    h = xn * (1.0 + sc_ref[...]) + sh_ref[...]
    tm = h.shape[0]
    nchunk = h.shape[1] // 128
    for c in range(nchunk):
        h_ref[pl.ds(c, tm, stride=nchunk), :] = h[:, c * 128:(c + 1) * 128]
    _router_topk(h, wr_ref[...], br_ref[...], ti_ref, tg_ref)


def _post(x, y, mod, ln_g, ln_b, w_router, b_router, n_blocks, n_x_blocks, ctx_row):
    b, _, d = x.shape
    rows = n_blocks * ROW_BLK
    blk = pl.BlockSpec((None, ROW_BLK, d), lambda b_, t: (b_, t, 0))
    lane_blk = pl.BlockSpec((None, ROW_BLK, 128), lambda b_, t: (b_, t, 0))
    row = pl.BlockSpec((1, d), lambda b_, t: (0, 0))
    ms = functools.partial(_mod_spec, n_x_blocks=n_x_blocks, ctx_row=ctx_row)
    wr = jnp.zeros((d, ROUTER_LANES), F32).at[:, :N_EXPERTS].set(w_router)
    br = jnp.full((1, ROUTER_LANES), ROUTER_PAD, F32).at[0, :N_EXPERTS].set(b_router)
    return pl.pallas_call(
        _post_kernel,
        out_shape=(
            jax.ShapeDtypeStruct((b, rows, d), F32),
            jax.ShapeDtypeStruct((b, rows * (d // 128), 128), F32),
            jax.ShapeDtypeStruct((b, rows, 128), jnp.int32),
            jax.ShapeDtypeStruct((b, rows, 128), F32),
        ),
        grid=(b, n_blocks),
        in_specs=[blk, blk, ms(2), row, row, ms(3), ms(4),
                  pl.BlockSpec((d, ROUTER_LANES), lambda b_, t: (0, 0)),
                  pl.BlockSpec((1, ROUTER_LANES), lambda b_, t: (0, 0))],
        out_specs=(blk, pl.BlockSpec((None, ROW_BLK * (d // 128), 128), lambda b_, t: (b_, t, 0)),
                   lane_blk, lane_blk),
        compiler_params=_params(("parallel", "parallel")),
        name="post_mixer",
    )(x, y, mod, ln_g.reshape(1, d), ln_b.reshape(1, d), mod, mod, wr, br)


def _gather_kernel(nu_ref, idx_ref, h_hbm, o_ref, buf, sem):
    i = pl.program_id(0)
    tm = o_ref.shape[0]
    nchunk = o_ref.shape[1] // 128
    nu = nu_ref[0]

    def issue(block, slot):
        base = block * tm

        def start(r, carry):
            tok = idx_ref[base + r]
            pltpu.make_async_copy(
                h_hbm.at[pl.ds(pl.multiple_of(tok * nchunk, nchunk), nchunk), :],
                buf.at[slot, pl.ds(pl.multiple_of(r * nchunk, nchunk), nchunk), :],
                sem.at[slot]).start()
            return carry

        lax.fori_loop(0, tm, start, 0, unroll=8)

    @pl.when(i == 0)
    def _():
        issue(0, 0)

    @pl.when(i + 1 < nu)
    def _():
        issue(i + 1, (i + 1) % 2)

    @pl.when(i < nu)
    def _():
        slot = i % 2
        pltpu.make_async_copy(buf.at[slot], buf.at[slot], sem.at[slot]).wait()
        for c in range(nchunk):
            o_ref[:, c * 128:(c + 1) * 128] = buf[slot, pl.ds(c, tm, stride=nchunk), :].astype(o_ref.dtype)

    @pl.when(i >= nu)
    def _():
        o_ref[...] = jnp.zeros_like(o_ref)


def _moe_gather(h, row_tok, n_used, n_blocks):
    d = D_MODEL
    nchunk = d // 128
    tm = MOE_TM
    return pl.pallas_call(
        _gather_kernel,
        out_shape=jax.ShapeDtypeStruct((n_blocks * tm, d), BF16),
        grid_spec=pltpu.PrefetchScalarGridSpec(
            num_scalar_prefetch=2,
            grid=(n_blocks,),
            in_specs=[pl.BlockSpec(memory_space=pl.ANY)],
            out_specs=pl.BlockSpec((tm, d), lambda i, nu, idx: (i, 0)),
            scratch_shapes=[pltpu.VMEM((2, tm * nchunk, 128), F32), pltpu.SemaphoreType.DMA((2,))],
        ),
        compiler_params=_params(("arbitrary",)),
        name="moe_gather",
    )(n_used, row_tok, h)


def _expert_changed(be_ref, i):
    return (i == 0) | (be_ref[i] != be_ref[jnp.maximum(i - 1, 0)])


def _moe_gu_kernel(be_ref, nu_ref, x_ref, w_ref, b_ref, perm_ref, o_ref, wbf):
    i = pl.program_id(1)
    tn = w_ref.shape[1]
    sub = perm_ref.shape[0]
    half = sub // 2

    @pl.when(i < nu_ref[0])
    def _():
        @pl.when(_expert_changed(be_ref, i))
        def _():
            for c in range(tn // sub):
                cs = slice(c * sub, (c + 1) * sub)
                wbf[:, cs] = _dot(w_ref[:, cs].astype(BF16), perm_ref[...]).astype(BF16)

        for c in range(tn // sub):
            cs = slice(c * sub, (c + 1) * sub)
            gu = _dot(x_ref[...], wbf[:, cs]) + b_ref[:, cs]
            g = jnp.minimum(gu[:, :half], SWIGLU_LIMIT)
            u = jnp.clip(gu[:, half:], -SWIGLU_LIMIT, SWIGLU_LIMIT)
            act = (u + 1.0) * (g * _sigmoid(SWIGLU_ALPHA * g))
            o_ref[:, c * half:(c + 1) * half] = act.astype(o_ref.dtype)

    @pl.when(i >= nu_ref[0])
    def _():
        o_ref[...] = jnp.zeros_like(o_ref)


def _moe_down_kernel(be_ref, nu_ref, a_ref, w_ref, b_ref, o_ref, wbf, stage):
    i = pl.program_id(1)

    @pl.when(i < nu_ref[0])
    def _():
        @pl.when(_expert_changed(be_ref, i))
        def _():
            wbf[...] = w_ref[...].astype(BF16)

        per = MM_SUB // 128
        tm, nck, _ = o_ref.shape
        for c in range(nck // per):
            cs = slice(c * MM_SUB, (c + 1) * MM_SUB)
            res = _dot(a_ref[...], wbf[:, cs]) + b_ref[:, cs]
            for k in range(per):
                stage[pl.ds(c * per + k, tm, stride=nck), :] = res[:, k * 128:(k + 1) * 128]
        o_ref[...] = stage[...].reshape(tm, nck, 128)

    @pl.when(i >= nu_ref[0])
    def _():
        o_ref[...] = jnp.zeros_like(o_ref)


def _moe_experts(xs, block_e, n_used, layer, w_gu, b_gu, w_down, b_down):
    n_rows, d = xs.shape
    tm, tn = MOE_TM, MOE_TN
    n_blocks = n_rows // tm
    de2 = w_gu.shape[-1]
    de = de2 // 2
    sub = MM_SUB
    half = sub // 2
    perm = np.zeros((sub, sub), np.float32)
    perm[2 * np.arange(half), np.arange(half)] = 1.0
    perm[2 * np.arange(half) + 1, half + np.arange(half)] = 1.0
    perm = jnp.asarray(perm, BF16)
    b_perm = b_gu.reshape(DEPTH, N_EXPERTS, de2 // sub, half, 2)
    b_perm = jnp.swapaxes(b_perm, -1, -2).reshape(DEPTH, N_EXPERTS, 1, de2)

    def row(i, nu):
        return jnp.minimum(i, nu[0] - 1)

    act = pl.pallas_call(
        _moe_gu_kernel,
        out_shape=jax.ShapeDtypeStruct((n_rows, de), BF16),
        grid_spec=pltpu.PrefetchScalarGridSpec(
            num_scalar_prefetch=2,
            grid=(de2 // tn, n_blocks),
            in_specs=[
                pl.BlockSpec((tm, d), lambda j, i, be, nu: (row(i, nu), 0)),
                pl.BlockSpec((None, None, d, tn), lambda j, i, be, nu: (layer, be[i], 0, j)),
                pl.BlockSpec((None, None, 1, tn), lambda j, i, be, nu: (layer, be[i], 0, j)),
                pl.BlockSpec((sub, sub), lambda j, i, be, nu: (0, 0)),
            ],
            out_specs=pl.BlockSpec((tm, tn // 2), lambda j, i, be, nu: (i, j)),
            scratch_shapes=[pltpu.VMEM((d, tn), BF16)],
        ),
        compiler_params=_params(("arbitrary", "arbitrary")),
        name="moe_gate_up",
    )(block_e, n_used, xs, w_gu, b_perm, perm)

    yb = pl.pallas_call(
        _moe_down_kernel,
        out_shape=jax.ShapeDtypeStruct((n_rows, d // 128, 128), F32),
        grid_spec=pltpu.PrefetchScalarGridSpec(
            num_scalar_prefetch=2,
            grid=(d // tn, n_blocks),
            in_specs=[
                pl.BlockSpec((tm, de), lambda j, i, be, nu: (row(i, nu), 0)),
                pl.BlockSpec((None, None, de, tn), lambda j, i, be, nu: (layer, be[i], 0, j)),
                pl.BlockSpec((None, None, 1, tn), lambda j, i, be, nu: (layer, be[i], 0, j)),
            ],
            out_specs=pl.BlockSpec((tm, tn // 128, 128), lambda j, i, be, nu: (i, j, 0)),
            scratch_shapes=[pltpu.VMEM((de, tn), BF16), pltpu.VMEM((tm * (tn // 128), 128), F32)],
        ),
        compiler_params=_params(("arbitrary", "arbitrary")),
        name="moe_down",
    )(block_e, n_used, act, w_down, b_down.reshape(DEPTH, N_EXPERTS, 1, d))
    return yb


def _combine_kernel(dest_ref, tg_ref, x_ref, gate_ref, lng_ref, lnb_ref, *rest, with_next):
    if with_next:
        sh_ref, sc_ref, yb_hbm, xo_ref, h_ref, buf, ybuf, sem = rest
    else:
        yb_hbm, xo_ref, buf, ybuf, sem = rest
    tmc = x_ref.shape[0]
    nchunk = x_ref.shape[1] // 128
    step = pl.program_id(0) * pl.num_programs(1) + pl.program_id(1)
    n_steps = pl.num_programs(0) * pl.num_programs(1)

    def issue(stp, slot):
        base = stp * (tmc * TOP_K)

        def start(t, carry):
            for kk in range(TOP_K):
                row = dest_ref[base + t * TOP_K + kk]
                pltpu.make_async_copy(
                    yb_hbm.at[pl.ds(pl.multiple_of(row * nchunk, nchunk), nchunk), :],
                    buf.at[slot, kk, pl.ds(pl.multiple_of(t * nchunk, nchunk), nchunk), :],
                    sem.at[slot]).start()
            return carry

        lax.fori_loop(0, tmc, start, 0, unroll=4)

    @pl.when(step == 0)
    def _():
        issue(0, 0)

    @pl.when(step + 1 < n_steps)
    def _():
        issue(step + 1, (step + 1) % 2)

    slot = step % 2
    pltpu.make_async_copy(buf.at[slot], buf.at[slot], sem.at[slot]).wait()

    tg = tg_ref[...]
    for c in range(nchunk):
        yc = tg[:, 0:1] * buf[slot, 0, pl.ds(c, tmc, stride=nchunk), :]
        for kk in range(1, TOP_K):
            yc = yc + tg[:, kk:kk + 1] * buf[slot, kk, pl.ds(c, tmc, stride=nchunk), :]
        ybuf[:, c * 128:(c + 1) * 128] = yc
    v = DN_ALPHA * x_ref[...] + gate_ref[...] * ybuf[...]
    xn = _layer_norm(v, lng_ref[...], lnb_ref[...])
    xo_ref[...] = xn
    if with_next:
        h_ref[...] = (xn * (1.0 + sc_ref[...]) + sh_ref[...]).astype(h_ref.dtype)


def _combine(x, yb, dest, tg, mod, ln_g, ln_b, n_x_blocks, ctx_row, mod_next=None):
    b, l, d = x.shape
    tmc = CMB_TM
    per = ROW_BLK // tmc
    with_next = mod_next is not None

    def mod_spec(section):
        def index(b_, t, dest_):
            return (jnp.where(t // per < n_x_blocks, b_, ctx_row), section, 0, 0)
        return pl.BlockSpec((None, None, 1, d), index)

    blk = pl.BlockSpec((None, tmc, d), lambda b_, t, dest_: (b_, t, 0))
    row = pl.BlockSpec((1, d), lambda b_, t, dest_: (0, 0))
    in_specs = [pl.BlockSpec((None, tmc, 128), lambda b_, t, dest_: (b_, t, 0)), blk, mod_spec(5), row, row]
    args = [tg, x, mod, ln_g.reshape(1, d), ln_b.reshape(1, d)]
    out_shape = [jax.ShapeDtypeStruct((b, l, d), F32)]
    out_specs = [blk]
    if with_next:
        in_specs += [mod_spec(0), mod_spec(1)]
        args += [mod_next, mod_next]
        out_shape.append(jax.ShapeDtypeStruct((b, l, d), BF16))
        out_specs.append(blk)
    in_specs.append(pl.BlockSpec(memory_space=pl.ANY))
    args.append(yb)
    return pl.pallas_call(
        functools.partial(_combine_kernel, with_next=with_next),
        out_shape=tuple(out_shape),
        grid_spec=pltpu.PrefetchScalarGridSpec(
            num_scalar_prefetch=1,
            grid=(b, l // tmc),
            in_specs=in_specs,
            out_specs=tuple(out_specs),
            scratch_shapes=[pltpu.VMEM((2, TOP_K, tmc * (d // 128), 128), F32), pltpu.VMEM((tmc, d), F32),
                            pltpu.SemaphoreType.DMA((2,))],
        ),
        compiler_params=_params(("arbitrary", "arbitrary")),
        name="moe_combine",
    )(dest, *args)


def _rank_kernel(ti_ref, tri_ref, rank_ref, cnt_ref, run):
    @pl.when(pl.program_id(0) == 0)
    def _():
        run[...] = jnp.zeros_like(run)

    ti = ti_ref[...]
    lane = lax.broadcasted_iota(jnp.int32, ti.shape, 1)
    onehot = [(lane == ti[:, k:k + 1]).astype(F32) for k in range(TOP_K)]
    per_token = onehot[0]
    for k in range(1, TOP_K):
        per_token = per_token + onehot[k]
    before = _dot(tri_ref[...], per_token.astype(BF16)) + run[...]
    rank = jnp.zeros(ti.shape, F32)
    for k in range(TOP_K):
        rank = jnp.where(lane == k, jnp.sum(onehot[k] * before, axis=-1, keepdims=True), rank)
        before = before + onehot[k]
    rank_ref[...] = rank.astype(jnp.int32)
    run[...] = run[...] + jnp.sum(per_token, axis=0, keepdims=True)
    cnt_ref[...] = run[...].astype(jnp.int32)


def _moe_rank(ti):
    t = ti.shape[0]
    tm = ROW_BLK
    tri = jnp.asarray(np.tril(np.ones((tm, tm), np.float32), -1), BF16)
    blk = pl.BlockSpec((tm, 128), lambda i: (i, 0))
    return pl.pallas_call(
        _rank_kernel,
        out_shape=(jax.ShapeDtypeStruct((t, 128), jnp.int32), jax.ShapeDtypeStruct((1, 128), jnp.int32)),
        grid=(t // tm,),
        in_specs=[blk, pl.BlockSpec((tm, tm), lambda i: (0, 0))],
        out_specs=(blk, pl.BlockSpec((1, 128), lambda i: (0, 0))),
        scratch_shapes=[pltpu.VMEM((1, 128), F32)],
        compiler_params=_params(("arbitrary",)),
        name="moe_rank",
    )(ti, tri)


def _moe_plan(ti):
    t = ti.shape[0]
    n_pairs = t * TOP_K
    tm = MOE_TM
    e_flat = ti[:, :TOP_K].reshape(-1)
    rank, counts = _moe_rank(ti)
    rank = rank[:, :TOP_K].reshape(-1)
    counts = counts[0, :N_EXPERTS]
    padded = (counts + tm - 1) // tm * tm
    pend = jnp.cumsum(padded)
    pstart = pend - padded
    dest = (pstart[e_flat] + rank).astype(jnp.int32)
    n_blocks = (n_pairs + N_EXPERTS * (tm - 1) + tm - 1) // tm
    n_rows = n_blocks * tm
    tok = jnp.arange(n_pairs, dtype=jnp.int32) // TOP_K
    row_tok = jnp.zeros((n_rows,), jnp.int32).at[dest].set(tok)
    n_used = (pend[-1] // tm).astype(jnp.int32)
    blk = jnp.arange(n_blocks, dtype=jnp.int32)
    block_e = jnp.searchsorted(pend, jnp.minimum(blk, n_used - 1) * tm, side="right")
    block_e = jnp.minimum(block_e, N_EXPERTS - 1).astype(jnp.int32)
    return dest, row_tok, block_e, n_used.reshape(1), n_blocks


def _moe(h, top_i, layer, w_gu, b_gu, w_down, b_down):
    dest, row_tok, block_e, n_used, n_blocks = _moe_plan(top_i)
    xs = _moe_gather(h, row_tok, n_used, n_blocks)
    yb = _moe_experts(xs, block_e, n_used, layer, w_gu, b_gu, w_down, b_down)
    return yb.reshape(-1, 128), dest


def _conv_ctx_kernel(u_ref, w_ref, b_ref, o_ref):
    u = u_ref[...]
    n = u.shape[0]
    row = lax.broadcasted_iota(jnp.int32, u.shape, 0)
    w = w_ref[...]
    um2 = jnp.where(row >= 2, pltpu.roll(u, 2, 0), 0.0)
    um1 = jnp.where(row >= 1, pltpu.roll(u, 1, 0), 0.0)
    up1 = jnp.where(row < n - 1, pltpu.roll(u, n - 1, 0), 0.0)
    o_ref[...] = w[0:1] * um2 + w[1:2] * um1 + w[2:3] * u + w[3:4] * up1 + b_ref[...]


def _conv_ctx(z1, conv_w, conv_b, n_x, n_ctx):
    b, l, _ = z1.shape
    w = conv_w.shape[1]
    return pl.pallas_call(
        _conv_ctx_kernel,
        out_shape=jax.ShapeDtypeStruct((b, n_ctx, w), F32),
        grid=(b,),
        in_specs=[
            pl.BlockSpec((None, n_ctx, w), lambda b_: (b_, n_x // n_ctx, 1)),
            pl.BlockSpec((CONV_W, w), lambda b_: (0, 0)),
            pl.BlockSpec((1, w), lambda b_: (0, 0)),
        ],
        out_specs=pl.BlockSpec((None, n_ctx, w), lambda b_: (b_, 0, 0)),
        compiler_params=_params(("parallel",)),
        name="lru_conv_ctx",
    )(z1, conv_w, conv_b.reshape(1, w))


def _conv_x_kernel(u_ref, prev_ref, next_ref, w_ref, b_ref, o_ref):
    g = pl.program_id(1)
    ng = pl.num_programs(1)
    r = u_ref.shape[0]
    wg = u_ref.shape[1]
    w = w_ref[...]
    c0, c1, c2, c3 = w[0:1], w[1:2], w[2:3], w[3:4]
    bias = b_ref[...]
    sub = lax.broadcasted_iota(jnp.int32, u_ref.shape[1:], 0)

    def from_prev_col(x, halo):
        return jnp.where(sub == 0, halo, pltpu.roll(x, 1, 0))

    def from_next_col(x, halo):
        return jnp.where(sub == wg - 1, halo, pltpu.roll(x, wg - 1, 0))

    has_prev = (g > 0).astype(F32)
    has_next = (g < ng - 1).astype(F32)
    hp = prev_ref.shape[0]
    u_m1 = from_prev_col(u_ref[r - 1], prev_ref[hp - 1, wg - 1:wg, :] * has_prev)
    u_m2 = from_prev_col(u_ref[r - 2], prev_ref[hp - 2, wg - 1:wg, :] * has_prev)
    u_p = from_next_col(u_ref[0], next_ref[0, 0:1, :] * has_next)

    def interior(i, carry):
        o_ref[i] = c0 * u_ref[i - 2] + c1 * u_ref[i - 1] + c2 * u_ref[i] + c3 * u_ref[i + 1] + bias
        return carry

    lax.fori_loop(2, r - 1, interior, 0)
    o_ref[0] = c0 * u_m2 + c1 * u_m1 + c2 * u_ref[0] + c3 * u_ref[1] + bias
    o_ref[1] = c0 * u_m1 + c1 * u_ref[0] + c2 * u_ref[1] + c3 * u_ref[2] + bias
    o_ref[r - 1] = c0 * u_ref[r - 3] + c1 * u_ref[r - 2] + c2 * u_ref[r - 1] + c3 * u_p + bias


def _conv_x(z1, conv_w, conv_b, n_x):
    b, l, f = z1.shape
    w = conv_w.shape[1]
    rows = n_x // GRID_W
    wg = W_GROUP
    ng = GRID_W // wg
    z4 = z1.reshape(b, l // GRID_W, GRID_W, f)
    halo = 8
    last_halo = rows // halo - 1
    return pl.pallas_call(
        _conv_x_kernel,
        out_shape=jax.ShapeDtypeStruct((b, rows, GRID_W, w), F32),
        grid=(b, ng),
        in_specs=[
            pl.BlockSpec((None, rows, wg, w), lambda b_, g: (b_, 0, g, 1)),
            pl.BlockSpec((None, halo, wg, w), lambda b_, g: (b_, last_halo, jnp.maximum(g - 1, 0), 1)),
            pl.BlockSpec((None, halo, wg, w), lambda b_, g: (b_, 0, jnp.minimum(g + 1, ng - 1), 1)),
            pl.BlockSpec((CONV_W, w), lambda b_, g: (0, 0)),
            pl.BlockSpec((1, w), lambda b_, g: (0, 0)),
        ],
        out_specs=pl.BlockSpec((None, rows, wg, w), lambda b_, g: (b_, 0, g, 0)),
        compiler_params=_params(("parallel", "parallel")),
        name="lru_conv_x",
    )(z4, z4, z4, conv_w, conv_b.reshape(1, w))


def _gates_kernel(u_ref, wa_ref, wx_ref, ba_ref, bx_ref, lam_ref, a_ref, b_ref, wabf, wxbf):
    @pl.when(pl.program_id(0) == 0)
    def _():
        wabf[...] = wa_ref[...].astype(BF16)
        wxbf[...] = wx_ref[...].astype(BF16)

    for d in range(2):
        lam = lam_ref[d]
        nl = -lam
        softplus = jnp.maximum(nl, 0.0) + jnp.log(1.0 + jnp.exp(-jnp.abs(nl)))
        for h in range(LRU_HEADS):
            sl = slice(h * LRU_BLOCK, (h + 1) * LRU_BLOCK)
            u = u_ref[:, sl]
            ub = u.astype(BF16)
            r = _sigmoid(_dot(ub, wabf[d, h]) + ba_ref[d][:, sl])
            ig = _sigmoid(_dot(ub, wxbf[d, h]) + bx_ref[d][:, sl])
            log_a = (-LRU_C) * r * softplus[:, sl]
            a = jnp.exp(log_a)
            a_ref[d, :, sl] = a
            b_ref[d, :, sl] = jnp.sqrt(1.0 - a * a) * (ig * u)


def _gates(cu, w_a, w_x, b_a, b_x, lam):
    m, w = cu.shape
    tm = ROW_BLK
    full5 = pl.BlockSpec(w_a.shape, lambda i: (0, 0, 0, 0))
    vec = pl.BlockSpec((2, 1, w), lambda i: (0, 0, 0))
    out = pl.BlockSpec((2, tm, w), lambda i: (0, i, 0))
    return pl.pallas_call(
        _gates_kernel,
        out_shape=(jax.ShapeDtypeStruct((2, m, w), F32), jax.ShapeDtypeStruct((2, m, w), F32)),
        grid=(m // tm,),
        in_specs=[pl.BlockSpec((tm, w), lambda i: (i, 0)), full5, full5, vec, vec, vec],
        out_specs=(out, out),
        scratch_shapes=[pltpu.VMEM(w_a.shape, BF16), pltpu.VMEM(w_a.shape, BF16)],
        compiler_params=_params(("arbitrary",)),
        name="lru_gates",
    )(cu, w_a, w_x, b_a.reshape(2, 1, w), b_x.reshape(2, 1, w), lam.reshape(2, 1, w))


def _lru_scan_kernel(ax_ref, bx_ref, ac_ref, bc_ref, h_ref, carry, acum, *, rev):
    s = pl.program_id(2)
    rows = ax_ref.shape[0]
    wg = ax_ref.shape[1]
    n_ctx = ac_ref.shape[0]
    ft = ax_ref.shape[2]

    @pl.when(s == 0)
    def _():
        def body(t, h):
            tt = n_ctx - 1 - t if rev else t
            return ac_ref[pl.ds(tt, 1), :] * h + bc_ref[pl.ds(tt, 1), :]

        carry[...] = lax.fori_loop(0, n_ctx, body, jnp.zeros((1, ft), F32))

    @pl.when(s > 0)
    def _():
        def seg(i, c):
            a_run, h_run = c
            r = rows - 1 - i if rev else i
            a = ax_ref[r]
            h_run = a * h_run + bx_ref[r]
            a_run = a_run * a
            h_ref[r] = h_run
            acum[r] = a_run
            return a_run, h_run

        a_tot, h_tot = lax.fori_loop(
            0, rows, seg, (jnp.ones((wg, ft), F32), jnp.zeros((wg, ft), F32)))
        h = carry[...]
        sub = lax.broadcasted_iota(jnp.int32, (wg, ft), 0)
        h_in = jnp.zeros((wg, ft), F32)
        for j in (range(wg - 1, -1, -1) if rev else range(wg)):
            h_in = jnp.where(sub == j, h, h_in)
            h = a_tot[j:j + 1] * h + h_tot[j:j + 1]
        carry[...] = h

        def fix(r, c):
            h_ref[r] = h_ref[r] + acum[r] * h_in
            return c

        lax.fori_loop(0, rows, fix, 0)


def _lru_scan(a_x, b_x, a_c, b_c, d, rev):
    _, b, rows, gw, w = a_x.shape
    n_ctx = a_c.shape[2]
    wg = W_GROUP
    ng = gw // wg
    ft = 1024

    def grp(s):
        k = jnp.maximum(s - 1, 0)
        return ng - 1 - k if rev else k

    xblk = pl.BlockSpec((None, None, rows, wg, ft), lambda b_, f, s: (d, b_, 0, grp(s), f))
    cblk = pl.BlockSpec((None, None, n_ctx, ft), lambda b_, f, s: (d, b_, 0, f))
    return pl.pallas_call(
        functools.partial(_lru_scan_kernel, rev=rev),
        out_shape=jax.ShapeDtypeStruct((b, rows, gw, w), F32),
        grid=(b, w // ft, ng + 1),
        in_specs=[xblk, xblk, cblk, cblk],
        out_specs=pl.BlockSpec((None, rows, wg, ft), lambda b_, f, s: (b_, 0, grp(s), f)),
        scratch_shapes=[pltpu.VMEM((1, ft), F32), pltpu.VMEM((rows, wg, ft), F32)],
        compiler_params=_params(("parallel", "parallel", "arbitrary")),
        name="lru_scan_bwd" if rev else "lru_scan_fwd",
    )(a_x, b_x, a_c, b_c)


def _lru_prep_kernel(y_ref, hf_ref, hb_ref, o_ref):
    o_ref[...] = (y_ref[...] * (hf_ref[...] + hb_ref[...])).astype(o_ref.dtype)


def _lru_prep(z1, h_f, h_b, n_x):
    b, l, _ = z1.shape
    w = h_f.shape[-1]
    blk = pl.BlockSpec((None, ROW_BLK, w), lambda b_, t: (b_, t, 0))
    return pl.pallas_call(
        _lru_prep_kernel,
        out_shape=jax.ShapeDtypeStruct((b, n_x, w), BF16),
        grid=(b, n_x // ROW_BLK),
        in_specs=[blk, blk, blk],
        out_specs=blk,
        compiler_params=_params(("parallel", "parallel")),
        name="lru_prep",
    )(z1, h_f, h_b)


def kernel(x, c, ctx, c_ctx, ada_w, ada_b, ln1_g, ln1_b, ln2_g, ln2_b, hg_w_in, hg_lb_logits, hg_norm_g, hg_w_out, lru_w_in, lru_conv_w, lru_conv_b, lru_w_a, lru_b_a, lru_w_x, lru_b_x, lru_lam, lru_w_out, moe_w_router, moe_b_router, moe_w_gu, moe_b_gu, moe_w_down, moe_b_down):
    bsz, n_x, d = x.shape
    n_ctx = ctx.shape[1]
    assert d == D_MODEL and ada_w.shape[0] == DEPTH == 2
    assert n_x % ROW_BLK == 0 and n_ctx == ROW_BLK and n_x % (GRID_W * 8) == 0 and bsz < 8
    l = n_x + n_ctx
    n_x_blocks = n_x // ROW_BLK
    n_blocks = l // ROW_BLK
    ctx_row = bsz

    cv = jnp.zeros((8, d), F32).at[:bsz].set(c).at[bsz].set(c_ctx)
    mod = _ada(cv, ada_w, ada_b).reshape(DEPTH, 8, 6, 1, d)
    xcat = jnp.concatenate([x, ctx], axis=1)

    h0 = _modulate(xcat, mod[0], n_x_blocks, ctx_row)
    lb = jnp.cumsum(jax.nn.softmax(hg_lb_logits.astype(F32), axis=1), axis=1)[:, 0]
    lb_row = jnp.zeros((1, 5 * d), F32).at[0, d:3 * d].set(lb.reshape(-1))
    z = _matmul(h0.reshape(bsz * l, d), hg_w_in[0], mode="hgrn_in", lb_row=lb_row, section=d)
    o2 = _hgrn_scan(z.reshape(bsz, l, 5 * d), n_x)
    a0 = _hgrn_prep(o2.reshape(2, bsz * l, d), z, hg_norm_g[0])
    y0 = _matmul(a0, hg_w_out[0]).reshape(bsz, l, d)
    x1, h1, ti, tg = _post(xcat, y0, mod[0], ln1_g[0], ln1_b[0], moe_w_router[0], moe_b_router[0],
                           n_blocks, n_x_blocks, ctx_row)
    yb, dest = _moe(h1.reshape(-1, 128), ti.reshape(bsz * l, 128), 0,
                    moe_w_gu, moe_b_gu, moe_w_down, moe_b_down)
    x2, h2 = _combine(x1, yb, dest, tg, mod[0], ln2_g[0], ln2_b[0], n_x_blocks, ctx_row, mod_next=mod[1])

    z1 = _matmul(h2.reshape(bsz * l, d), lru_w_in[0], mode="lru_in", section=d).reshape(bsz, l, 2 * d)
    cu_x = _conv_x(z1, lru_conv_w[0], lru_conv_b[0], n_x)
    cu_c = _conv_ctx(z1, lru_conv_w[0], lru_conv_b[0], n_x, n_ctx)
    rows = n_x // GRID_W
    a_x, b_x = _gates(cu_x.reshape(bsz * n_x, d), lru_w_a[0], lru_w_x[0], lru_b_a[0], lru_b_x[0], lru_lam[0])
    a_c, b_c = _gates(cu_c.reshape(bsz * n_ctx, d), lru_w_a[0], lru_w_x[0], lru_b_a[0], lru_b_x[0], lru_lam[0])
    a_x = a_x.reshape(2, bsz, rows, GRID_W, d)
    b_x = b_x.reshape(2, bsz, rows, GRID_W, d)
    a_c = a_c.reshape(2, bsz, n_ctx, d)
    b_c = b_c.reshape(2, bsz, n_ctx, d)
    h_f = _lru_scan(a_x, b_x, a_c, b_c, 0, rev=False).reshape(bsz, n_x, d)
    h_b = _lru_scan(a_x, b_x, a_c, b_c, 1, rev=True).reshape(bsz, n_x, d)
    a1 = _lru_prep(z1, h_f, h_b, n_x)
    y1 = _matmul(a1.reshape(bsz * n_x, d), lru_w_out[0]).reshape(bsz, n_x, d)
    x3, h3, ti, tg = _post(x2, y1, mod[1], ln1_g[1], ln1_b[1], moe_w_router[1], moe_b_router[1],
                           n_x_blocks, n_x_blocks, ctx_row)
    yb, dest = _moe(h3.reshape(-1, 128), ti.reshape(bsz * n_x, 128), 1,
                    moe_w_gu, moe_b_gu, moe_w_down, moe_b_down)
    (out,) = _combine(x3, yb, dest, tg, mod[1], ln2_g[1], ln2_b[1], n_x_blocks, ctx_row)
    return out
```

```python
import functools

import numpy as np
import jax
import jax.numpy as jnp
from jax import lax
from jax.experimental import pallas as pl
from jax.experimental.pallas import tpu as pltpu

F32 = jnp.float32
BF16 = jnp.bfloat16

D_MODEL = 2048
DEPTH = 2
GRID_W = 64
HG_HEADS = 16
HG_DK = 128
LRU_HEADS = 8
LRU_BLOCK = D_MODEL // LRU_HEADS
CONV_W = 4
LRU_C = 8.0
N_EXPERTS = 32
TOP_K = 4
SWIGLU_LIMIT = 7.0
SWIGLU_ALPHA = 1.702
DN_ALPHA = float((2 * DEPTH) ** 0.25)
LN_EPS = 1e-5
RMS_EPS = 1e-6
ROUTER_LANES = 128
ROUTER_PAD = -1e30

ROW_BLK = 256
MM_TM = 1024
MM_TN = 1024
MM_SUB = 256
HG_CHUNK = 128
HG_GROUP = 16
MOE_TM = 512
MOE_TN = 1024
CMB_TM = 128
DMA_UNROLL = 8
W_GROUP = 8
VMEM_LIMIT = 56 * 1024 * 1024


def _params(sem):
    return pltpu.CompilerParams(dimension_semantics=sem, vmem_limit_bytes=VMEM_LIMIT)


def _split_bf16(a):
    hi = a.astype(BF16)
    lo = (a - hi.astype(F32)).astype(BF16)
    return hi, lo


def _dot(a, b):
    return jnp.dot(a, b, preferred_element_type=F32)


def _dot3(a, b):
    ah, al = _split_bf16(a)
    bh, bl = _split_bf16(b)
    return _dot(ah, bh) + _dot(ah, bl) + _dot(al, bh)


def _dot_nt(a, b):
    return lax.dot_general(a, b, (((1,), (1,)), ((), ())), preferred_element_type=F32)


def _dot_tn(a, b):
    return lax.dot_general(a, b, (((0,), (0,)), ((), ())), preferred_element_type=F32)


def _sigmoid(x):
    return 0.5 * (1.0 + jnp.tanh(0.5 * x))


def _ada_kernel(cv_ref, w_ref, b_ref, o_ref):
    x = cv_ref[...]
    s = x * _sigmoid(x)
    o_ref[...] = _dot3(s, w_ref[...]) + b_ref[...]


def _ada(cv, ada_w, ada_b):
    depth, d, n = ada_w.shape
    tn = 1024
    return pl.pallas_call(
        _ada_kernel,
        out_shape=jax.ShapeDtypeStruct((depth, 8, n), F32),
        grid=(depth, n // tn),
        in_specs=[
            pl.BlockSpec((8, d), lambda l, j: (0, 0)),
            pl.BlockSpec((None, d, tn), lambda l, j: (l, 0, j)),
            pl.BlockSpec((None, 1, tn), lambda l, j: (l, 0, j)),
        ],
        out_specs=pl.BlockSpec((None, 8, tn), lambda l, j: (l, 0, j)),
        compiler_params=_params(("arbitrary", "arbitrary")),
        name="ada_mod",
    )(cv, ada_w, ada_b.reshape(depth, 1, n))


def _mod_spec(section, n_x_blocks, ctx_row):
    def index(b, t):
        return (jnp.where(t < n_x_blocks, b, ctx_row), section, 0, 0)
    return pl.BlockSpec((None, None, 1, D_MODEL), index)


def _modulate_kernel(x_ref, sh_ref, sc_ref, o_ref):
    o_ref[...] = (x_ref[...] * (1.0 + sc_ref[...]) + sh_ref[...]).astype(o_ref.dtype)


def _modulate(xcat, mod, n_x_blocks, ctx_row):
    b, l, d = xcat.shape
    blk = pl.BlockSpec((None, ROW_BLK, d), lambda b_, t: (b_, t, 0))
    return pl.pallas_call(
        _modulate_kernel,
        out_shape=jax.ShapeDtypeStruct((b, l, d), BF16),
        grid=(b, l // ROW_BLK),
        in_specs=[blk, _mod_spec(0, n_x_blocks, ctx_row), _mod_spec(1, n_x_blocks, ctx_row)],
        out_specs=blk,
        compiler_params=_params(("parallel", "parallel")),
        name="modulate",
    )(xcat, mod, mod)


def _gelu_tanh(x):
    return 0.5 * x * (1.0 + jnp.tanh(0.7978845608028654 * (x + 0.044715 * (x * x * x))))


def _mm_kernel(a_ref, w_ref, *rest, mode, tiles_per_section):
    o_ref, wbf = rest[-2], rest[-1]

    @pl.when(pl.program_id(1) == 0)
    def _():
        wbf[...] = w_ref[...].astype(BF16)

    sec = pl.program_id(0) // tiles_per_section
    tn = o_ref.shape[1]

    def tile(epilogue):
        for c in range(tn // MM_SUB):
            cs = slice(c * MM_SUB, (c + 1) * MM_SUB)
            o_ref[:, cs] = epilogue(_dot(a_ref[...], wbf[:, cs]), cs)

    def identity(acc, cs):
        return acc

    if mode == "plain":
        tile(identity)
    elif mode == "hgrn_in":
        lb_ref = rest[0]

        @pl.when((sec == 0) | (sec == 4))
        def _():
            tile(lambda acc, cs: acc * _sigmoid(acc))

        @pl.when((sec == 1) | (sec == 2))
        def _():
            tile(lambda acc, cs: lb_ref[:, cs] + (1.0 - lb_ref[:, cs]) * _sigmoid(acc))

        @pl.when(sec == 3)
        def _():
            tile(identity)
    elif mode == "lru_in":
        @pl.when(sec == 0)
        def _():
            tile(lambda acc, cs: _gelu_tanh(acc))

        @pl.when(sec == 1)
        def _():
            tile(identity)
    else:
        raise ValueError(mode)


def _matmul(a, w, mode="plain", lb_row=None, section=None):
    m, k = a.shape
    n = w.shape[1]
    tm = MM_TM if m % MM_TM == 0 else ROW_BLK
    tn = MM_TN
    section = section or n
    in_specs = [
        pl.BlockSpec((tm, k), lambda j, i: (i, 0)),
        pl.BlockSpec((k, tn), lambda j, i: (0, j)),
    ]
    args = [a, w]
    if lb_row is not None:
        in_specs.append(pl.BlockSpec((1, tn), lambda j, i: (0, j)))
        args.append(lb_row)
    return pl.pallas_call(
        functools.partial(_mm_kernel, mode=mode, tiles_per_section=section // tn),
        out_shape=jax.ShapeDtypeStruct((m, n), F32),
        grid=(n // tn, m // tm),
        in_specs=in_specs,
        out_specs=pl.BlockSpec((tm, tn), lambda j, i: (i, j)),
        scratch_shapes=[pltpu.VMEM((k, tn), BF16)],
        compiler_params=_params(("arbitrary", "arbitrary")),
        name="matmul_" + mode,
    )(*args)


def _hgrn_tables(c):
    levels = []
    bs = c
    while bs >= 2:
        levels.append(bs)
        bs //= 2
    nlev = len(levels)
    g = np.zeros((2, (2 + nlev) * c, c), np.float32)
    masks = np.zeros((2, nlev + 1, c, c), np.float32)
    r = np.arange(c)
    for d in range(2):
        for row in range(c):
            if d == 0:
                g[d, row, : row + 1] = 1.0
                g[d, c + row, row + 1:] = 1.0
            else:
                g[d, row, row:] = 1.0
                g[d, c + row, :row] = 1.0
        for li, bs in enumerate(levels):
            half = bs // 2
            for row in range(c):
                base = (row // bs) * bs
                mid = base + half
                pos = row - base
                blk = g[d, (2 + li) * c + row]
                if d == 0:
                    if pos >= half:
                        blk[mid: row + 1] = 1.0
                    else:
                        blk[row + 1: mid] = 1.0
                else:
                    if pos < half:
                        blk[row: mid] = 1.0
                    else:
                        blk[mid: row] = 1.0
            same = (r[:, None] // bs) == (r[None, :] // bs)
            t_hi = (r[:, None] % bs) >= half
            s_hi = (r[None, :] % bs) >= half
            if d == 0:
                masks[d, li] = same & t_hi & ~s_hi
            else:
                masks[d, li] = same & ~t_hi & s_hi
        masks[d, nlev] = np.eye(c)
    return jnp.asarray(g, BF16), jnp.asarray(masks, F32), nlev


def _hgrn_scan_kernel(q_ref, f_ref, v_ref, g_ref, m_ref, o_ref, st_ref, *, nlev):
    c = q_ref.shape[0]

    @pl.when(pl.program_id(3) == 0)
    def _():
        st_ref[...] = jnp.zeros_like(st_ref)

    gmat = g_ref[...]

    for h in range(HG_GROUP):
        sl = slice(h * HG_DK, (h + 1) * HG_DK)
        f = f_ref[:, sl]
        q = q_ref[:, sl]
        vb = v_ref[:, sl].astype(BF16)
        lf = jnp.log(f)
        k = 1.0 - f
        lf_hi, lf_lo = _split_bf16(lf)
        ex2 = _dot(gmat, jnp.concatenate([lf_hi, lf_lo], axis=1))
        ex = ex2[:, :HG_DK] + ex2[:, HG_DK:]
        e = jnp.exp(ex)
        st = st_ref[h]
        o = _dot_nt((q * e[0:c]).astype(BF16), st.astype(BF16))
        scores = m_ref[nlev] * _dot_nt(q.astype(BF16), k.astype(BF16))
        for li in range(nlev):
            el = e[(2 + li) * c:(3 + li) * c]
            scores = scores + m_ref[li] * _dot_nt((q * el).astype(BF16), (k * el).astype(BF16))
        o = o + _dot(scores.astype(BF16), vb)
        o_ref[:, sl] = o
        total = ex[0:1] + ex[c:c + 1]
        st_ref[h] = st * jnp.exp(total) + _dot_tn(vb, (k * e[c:2 * c]).astype(BF16))


def _hgrn_scan(z, n_x):
    b, l, _ = z.shape
    c = HG_CHUNK
    nc = l // c
    ncx = n_x // c
    gmat, masks, nlev = _hgrn_tables(c)

    def chunk(d, s):
        fwd = jnp.where(s < nc - ncx, ncx + s, s - (nc - ncx))
        return jnp.where(d == 0, fwd, nc - 1 - s)

    gw = HG_GROUP * HG_DK
    ngrp = D_MODEL // gw

    def spec(section):
        return pl.BlockSpec((None, c, gw), lambda b_, d, g, s: (b_, chunk(d, s), section(d) * ngrp + g))

    return pl.pallas_call(
        functools.partial(_hgrn_scan_kernel, nlev=nlev),
        out_shape=jax.ShapeDtypeStruct((2, b, l, D_MODEL), F32),
        grid=(b, 2, ngrp, nc),
        in_specs=[
            spec(lambda d: 0),
            spec(lambda d: 1 + d),
            spec(lambda d: 3),
            pl.BlockSpec((None,) + gmat.shape[1:], lambda b_, d, g, s: (d, 0, 0)),
            pl.BlockSpec((None,) + masks.shape[1:], lambda b_, d, g, s: (d, 0, 0, 0)),
        ],
        out_specs=pl.BlockSpec((None, None, c, gw), lambda b_, d, g, s: (d, b_, chunk(d, s), g)),
        scratch_shapes=[pltpu.VMEM((HG_GROUP, HG_DK, HG_DK), F32)],
        compiler_params=_params(("parallel", "arbitrary", "arbitrary", "arbitrary")),
        name="hgrn_scan",
    )(z, z, z, gmat, masks)


def _hgrn_prep_kernel(of_ref, ob_ref, g_ref, ng_ref, o_ref):
    ng = ng_ref[...]
    for h in range(HG_HEADS):
        sl = slice(h * HG_DK, (h + 1) * HG_DK)
        o = of_ref[:, sl] + ob_ref[:, sl]
        ms = jnp.mean(o * o, axis=-1, keepdims=True)
        y = o * lax.rsqrt(ms + RMS_EPS) * ng
        o_ref[:, sl] = (y * g_ref[:, sl]).astype(o_ref.dtype)


def _hgrn_prep(o2, z, norm_g):
    _, m, d = o2.shape
    tm = 512 if m % 512 == 0 else ROW_BLK
    return pl.pallas_call(
        _hgrn_prep_kernel,
        out_shape=jax.ShapeDtypeStruct((m, d), BF16),
        grid=(m // tm,),
        in_specs=[
            pl.BlockSpec((None, tm, d), lambda i: (0, i, 0)),
            pl.BlockSpec((None, tm, d), lambda i: (1, i, 0)),
            pl.BlockSpec((tm, d), lambda i: (i, 4)),
            pl.BlockSpec((1, HG_DK), lambda i: (0, 0)),
        ],
        out_specs=pl.BlockSpec((tm, d), lambda i: (i, 0)),
        compiler_params=_params(("parallel",)),
        name="hgrn_prep",
    )(o2, o2, z, norm_g.reshape(1, HG_DK))


def _layer_norm(v, g, b):
    mu = jnp.mean(v, axis=-1, keepdims=True)
    vc = v - mu
    var = jnp.mean(vc * vc, axis=-1, keepdims=True)
    return vc * lax.rsqrt(var + LN_EPS) * g + b


def _router_topk(h, wr, br, ti_ref, tg_ref):
    logits = _dot3(h, wr) + br
    lane = lax.broadcasted_iota(jnp.int32, logits.shape, 1)
    lane_f = lane.astype(F32)
    vals, ids = [], []
    cur = logits
    for _ in range(TOP_K):
        mx = jnp.max(cur, axis=-1, keepdims=True)
        idx = jnp.min(jnp.where(cur == mx, lane_f, float(ROUTER_LANES)), axis=-1, keepdims=True)
        vals.append(mx)
        ids.append(idx)
        cur = jnp.where(lane_f == idx, -jnp.inf, cur)
    ex = [jnp.exp(v - vals[0]) for v in vals]
    den = ex[0] + ex[1] + ex[2] + ex[3]
    ti = jnp.zeros(logits.shape, F32)
    tg = jnp.zeros(logits.shape, F32)
    for kk in range(TOP_K):
        ti = jnp.where(lane == kk, ids[kk], ti)
        tg = jnp.where(lane == kk, ex[kk] / den, tg)
    ti_ref[...] = ti.astype(jnp.int32)
    tg_ref[...] = tg


def _post_kernel(x_ref, y_ref, gate_ref, lng_ref, lnb_ref, sh_ref, sc_ref, wr_ref, br_ref,
                 xo_ref, h_ref, ti_ref, tg_ref):
    v = DN_ALPHA * x_ref[...] + gate_ref[...] * y_ref[...]
    xn = _layer_norm(v, lng_ref[...], lnb_ref[...])
    xo_ref[...] = xn
    h = xn * (1.0 + sc_ref[...]) + sh_ref[...]
    h_ref[...] = h
    _router_topk(h, wr_ref[...], br_ref[...], ti_ref, tg_ref)


def _post(x, y, mod, ln_g, ln_b, w_router, b_router, n_blocks, n_x_blocks, ctx_row):
    b, _, d = x.shape
    rows = n_blocks * ROW_BLK
    blk = pl.BlockSpec((None, ROW_BLK, d), lambda b_, t: (b_, t, 0))
    lane_blk = pl.BlockSpec((None, ROW_BLK, 128), lambda b_, t: (b_, t, 0))
    row = pl.BlockSpec((1, d), lambda b_, t: (0, 0))
    ms = functools.partial(_mod_spec, n_x_blocks=n_x_blocks, ctx_row=ctx_row)
    wr = jnp.zeros((d, ROUTER_LANES), F32).at[:, :N_EXPERTS].set(w_router)
    br = jnp.full((1, ROUTER_LANES), ROUTER_PAD, F32).at[0, :N_EXPERTS].set(b_router)
    return pl.pallas_call(
        _post_kernel,
        out_shape=(
            jax.ShapeDtypeStruct((b, rows, d), F32),
            jax.ShapeDtypeStruct((b, rows, d), F32),
            jax.ShapeDtypeStruct((b, rows, 128), jnp.int32),
            jax.ShapeDtypeStruct((b, rows, 128), F32),
        ),
        grid=(b, n_blocks),
        in_specs=[blk, blk, ms(2), row, row, ms(3), ms(4),
                  pl.BlockSpec((d, ROUTER_LANES), lambda b_, t: (0, 0)),
                  pl.BlockSpec((1, ROUTER_LANES), lambda b_, t: (0, 0))],
        out_specs=(blk, blk, lane_blk, lane_blk),
        compiler_params=_params(("parallel", "parallel")),
        name="post_mixer",
    )(x, y, mod, ln_g.reshape(1, d), ln_b.reshape(1, d), mod, mod, wr, br)


def _gather_kernel(nu_ref, idx_ref, h_hbm, o_ref, buf, sem):
    i = pl.program_id(0)
    tm = o_ref.shape[0]
    nu = nu_ref[0]

    def issue(block, slot):
        base = block * tm

        def start(r8, carry):
            for u in range(DMA_UNROLL):
                r = r8 * DMA_UNROLL + u
                tok = idx_ref[base + r]
                pltpu.make_async_copy(
                    h_hbm.at[pl.ds(tok, 1), :], buf.at[slot, pl.ds(r, 1), :], sem.at[slot]
                ).start(priority=u % 2)
            return carry

        lax.fori_loop(0, tm // DMA_UNROLL, start, 0)

    @pl.when(i == 0)
    def _():
        issue(0, 0)

    @pl.when(i + 1 < nu)
    def _():
        issue(i + 1, (i + 1) % 2)

    @pl.when(i < nu)
    def _():
        slot = i % 2
        pltpu.make_async_copy(buf.at[slot], buf.at[slot], sem.at[slot]).wait()
        o_ref[...] = buf[slot].astype(o_ref.dtype)

    @pl.when(i >= nu)
    def _():
        o_ref[...] = jnp.zeros_like(o_ref)


def _moe_gather(h, row_tok, n_used, n_blocks):
    t, d = h.shape
    tm = MOE_TM
    return pl.pallas_call(
        _gather_kernel,
        out_shape=jax.ShapeDtypeStruct((n_blocks * tm, d), BF16),
        grid_spec=pltpu.PrefetchScalarGridSpec(
            num_scalar_prefetch=2,
            grid=(n_blocks,),
            in_specs=[pl.BlockSpec(memory_space=pl.ANY)],
            out_specs=pl.BlockSpec((tm, d), lambda i, nu, idx: (i, 0)),
            scratch_shapes=[pltpu.VMEM((2, tm, d), F32), pltpu.SemaphoreType.DMA((2,))],
        ),
        compiler_params=_params(("arbitrary",)),
        name="moe_gather",
    )(n_used, row_tok, h)


def _expert_changed(be_ref, i):
    return (i == 0) | (be_ref[i] != be_ref[jnp.maximum(i - 1, 0)])


def _moe_gu_kernel(be_ref, nu_ref, x_ref, w_ref, b_ref, perm_ref, o_ref, wbf):
    i = pl.program_id(1)
    tn = w_ref.shape[1]
    sub = perm_ref.shape[0]
    half = sub // 2

    @pl.when(i < nu_ref[0])
    def _():
        @pl.when(_expert_changed(be_ref, i))
        def _():
            for c in range(tn // sub):
                cs = slice(c * sub, (c + 1) * sub)
                wbf[:, cs] = _dot(w_ref[:, cs].astype(BF16), perm_ref[...]).astype(BF16)

        for c in range(tn // sub):
            cs = slice(c * sub, (c + 1) * sub)
            gu = _dot(x_ref[...], wbf[:, cs]) + b_ref[:, cs]
            g = jnp.minimum(gu[:, :half], SWIGLU_LIMIT)
            u = jnp.clip(gu[:, half:], -SWIGLU_LIMIT, SWIGLU_LIMIT)
            act = (u + 1.0) * (g * _sigmoid(SWIGLU_ALPHA * g))
            o_ref[:, c * half:(c + 1) * half] = act.astype(o_ref.dtype)

    @pl.when(i >= nu_ref[0])
    def _():
        o_ref[...] = jnp.zeros_like(o_ref)


def _moe_down_kernel(be_ref, nu_ref, a_ref, w_ref, b_ref, o_ref, wbf):
    i = pl.program_id(1)

    @pl.when(i < nu_ref[0])
    def _():
        @pl.when(_expert_changed(be_ref, i))
        def _():
            wbf[...] = w_ref[...].astype(BF16)

        o_ref[...] = _dot(a_ref[...], wbf[...]) + b_ref[...]

    @pl.when(i >= nu_ref[0])
    def _():
        o_ref[...] = jnp.zeros_like(o_ref)


def _moe_experts(xs, block_e, n_used, layer, w_gu, b_gu, w_down, b_down):
    n_rows, d = xs.shape
    tm, tn = MOE_TM, MOE_TN
    n_blocks = n_rows // tm
    de2 = w_gu.shape[-1]
    de = de2 // 2
    sub = MM_SUB
    half = sub // 2
    perm = np.zeros((sub, sub), np.float32)
    perm[2 * np.arange(half), np.arange(half)] = 1.0
    perm[2 * np.arange(half) + 1, half + np.arange(half)] = 1.0
    perm = jnp.asarray(perm, BF16)
    b_perm = b_gu.reshape(DEPTH, N_EXPERTS, de2 // sub, half, 2)
    b_perm = jnp.swapaxes(b_perm, -1, -2).reshape(DEPTH, N_EXPERTS, 1, de2)

    def row(i, nu):
        return jnp.minimum(i, nu[0] - 1)

    act = pl.pallas_call(
        _moe_gu_kernel,
        out_shape=jax.ShapeDtypeStruct((n_rows, de), BF16),
        grid_spec=pltpu.PrefetchScalarGridSpec(
            num_scalar_prefetch=2,
            grid=(de2 // tn, n_blocks),
            in_specs=[
                pl.BlockSpec((tm, d), lambda j, i, be, nu: (row(i, nu), 0)),
                pl.BlockSpec((None, None, d, tn), lambda j, i, be, nu: (layer, be[i], 0, j)),
                pl.BlockSpec((None, None, 1, tn), lambda j, i, be, nu: (layer, be[i], 0, j)),
                pl.BlockSpec((sub, sub), lambda j, i, be, nu: (0, 0)),
            ],
            out_specs=pl.BlockSpec((tm, tn // 2), lambda j, i, be, nu: (i, j)),
            scratch_shapes=[pltpu.VMEM((d, tn), BF16)],
        ),
        compiler_params=_params(("arbitrary", "arbitrary")),
        name="moe_gate_up",
    )(block_e, n_used, xs, w_gu, b_perm, perm)

    yb = pl.pallas_call(
        _moe_down_kernel,
        out_shape=jax.ShapeDtypeStruct((n_rows, d), F32),
        grid_spec=pltpu.PrefetchScalarGridSpec(
            num_scalar_prefetch=2,
            grid=(d // tn, n_blocks),
            in_specs=[
                pl.BlockSpec((tm, de), lambda j, i, be, nu: (row(i, nu), 0)),
                pl.BlockSpec((None, None, de, tn), lambda j, i, be, nu: (layer, be[i], 0, j)),
                pl.BlockSpec((None, None, 1, tn), lambda j, i, be, nu: (layer, be[i], 0, j)),
            ],
            out_specs=pl.BlockSpec((tm, tn), lambda j, i, be, nu: (i, j)),
            scratch_shapes=[pltpu.VMEM((de, tn), BF16)],
        ),
        compiler_params=_params(("arbitrary", "arbitrary")),
        name="moe_down",
    )(block_e, n_used, act, w_down, b_down.reshape(DEPTH, N_EXPERTS, 1, d))
    return yb


def _combine_kernel(dest_ref, tg_ref, x_ref, gate_ref, lng_ref, lnb_ref, *rest, with_next):
    if with_next:
        sh_ref, sc_ref, yb_hbm, xo_ref, h_ref, buf, sem = rest
    else:
        yb_hbm, xo_ref, buf, sem = rest
    tmc = x_ref.shape[0]
    step = pl.program_id(0) * pl.num_programs(1) + pl.program_id(1)
    n_steps = pl.num_programs(0) * pl.num_programs(1)

    def issue(stp, slot):
        base = stp * (tmc * TOP_K)
        per_iter = DMA_UNROLL // TOP_K

        def start(t2, carry):
            for u in range(per_iter):
                t = t2 * per_iter + u
                for kk in range(TOP_K):
                    row = dest_ref[base + t * TOP_K + kk]
                    pltpu.make_async_copy(
                        yb_hbm.at[pl.ds(row, 1), :], buf.at[slot, kk, pl.ds(t, 1), :], sem.at[slot]
                    ).start(priority=kk % 2)
            return carry

        lax.fori_loop(0, tmc // per_iter, start, 0)

    @pl.when(step == 0)
    def _():
        issue(0, 0)

    @pl.when(step + 1 < n_steps)
    def _():
        issue(step + 1, (step + 1) % 2)

    slot = step % 2
    pltpu.make_async_copy(buf.at[slot], buf.at[slot], sem.at[slot]).wait()

    tg = tg_ref[...]
    y = tg[:, 0:1] * buf[slot, 0]
    for kk in range(1, TOP_K):
        y = y + tg[:, kk:kk + 1] * buf[slot, kk]
    v = DN_ALPHA * x_ref[...] + gate_ref[...] * y
    xn = _layer_norm(v, lng_ref[...], lnb_ref[...])
    xo_ref[...] = xn
    if with_next:
        h_ref[...] = (xn * (1.0 + sc_ref[...]) + sh_ref[...]).astype(h_ref.dtype)


def _combine(x, yb, dest, tg, mod, ln_g, ln_b, n_x_blocks, ctx_row, mod_next=None):
    b, l, d = x.shape
    tmc = CMB_TM
    per = ROW_BLK // tmc
    with_next = mod_next is not None

    def mod_spec(section):
        def index(b_, t, dest_):
            return (jnp.where(t // per < n_x_blocks, b_, ctx_row), section, 0, 0)
        return pl.BlockSpec((None, None, 1, d), index)

    blk = pl.BlockSpec((None, tmc, d), lambda b_, t, dest_: (b_, t, 0))
    row = pl.BlockSpec((1, d), lambda b_, t, dest_: (0, 0))
    in_specs = [pl.BlockSpec((None, tmc, 128), lambda b_, t, dest_: (b_, t, 0)), blk, mod_spec(5), row, row]
    args = [tg, x, mod, ln_g.reshape(1, d), ln_b.reshape(1, d)]
    out_shape = [jax.ShapeDtypeStruct((b, l, d), F32)]
    out_specs = [blk]
    if with_next:
        in_specs += [mod_spec(0), mod_spec(1)]
        args += [mod_next, mod_next]
        out_shape.append(jax.ShapeDtypeStruct((b, l, d), BF16))
        out_specs.append(blk)
    in_specs.append(pl.BlockSpec(memory_space=pl.ANY))
    args.append(yb)
    return pl.pallas_call(
        functools.partial(_combine_kernel, with_next=with_next),
        out_shape=tuple(out_shape),
        grid_spec=pltpu.PrefetchScalarGridSpec(
            num_scalar_prefetch=1,
            grid=(b, l // tmc),
            in_specs=in_specs,
            out_specs=tuple(out_specs),
            scratch_shapes=[pltpu.VMEM((2, TOP_K, tmc, d), F32), pltpu.SemaphoreType.DMA((2,))],
        ),
        compiler_params=_params(("arbitrary", "arbitrary")),
        name="moe_combine",
    )(dest, *args)


def _rank_kernel(ti_ref, tri_ref, rank_ref, cnt_ref, run):
    @pl.when(pl.program_id(0) == 0)
    def _():
        run[...] = jnp.zeros_like(run)

    ti = ti_ref[...]
    lane = lax.broadcasted_iota(jnp.int32, ti.shape, 1)
    onehot = [(lane == ti[:, k:k + 1]).astype(F32) for k in range(TOP_K)]
    per_token = onehot[0]
    for k in range(1, TOP_K):
        per_token = per_token + onehot[k]
    before = _dot(tri_ref[...], per_token.astype(BF16)) + run[...]
    rank = jnp.zeros(ti.shape, F32)
    for k in range(TOP_K):
        rank = jnp.where(lane == k, jnp.sum(onehot[k] * before, axis=-1, keepdims=True), rank)
        before = before + onehot[k]
    rank_ref[...] = rank.astype(jnp.int32)
    run[...] = run[...] + jnp.sum(per_token, axis=0, keepdims=True)
    cnt_ref[...] = run[...].astype(jnp.int32)


def _moe_rank(ti):
    t = ti.shape[0]
    tm = ROW_BLK
    tri = jnp.asarray(np.tril(np.ones((tm, tm), np.float32), -1), BF16)
    blk = pl.BlockSpec((tm, 128), lambda i: (i, 0))
    return pl.pallas_call(
        _rank_kernel,
        out_shape=(jax.ShapeDtypeStruct((t, 128), jnp.int32), jax.ShapeDtypeStruct((1, 128), jnp.int32)),
        grid=(t // tm,),
        in_specs=[blk, pl.BlockSpec((tm, tm), lambda i: (0, 0))],
        out_specs=(blk, pl.BlockSpec((1, 128), lambda i: (0, 0))),
        scratch_shapes=[pltpu.VMEM((1, 128), F32)],
        compiler_params=_params(("arbitrary",)),
        name="moe_rank",
    )(ti, tri)


def _moe_plan(ti):
    t = ti.shape[0]
    n_pairs = t * TOP_K
    tm = MOE_TM
    e_flat = ti[:, :TOP_K].reshape(-1)
    rank, counts = _moe_rank(ti)
    rank = rank[:, :TOP_K].reshape(-1)
    counts = counts[0, :N_EXPERTS]
    padded = (counts + tm - 1) // tm * tm
    pend = jnp.cumsum(padded)
    pstart = pend - padded
    dest = (pstart[e_flat] + rank).astype(jnp.int32)
    n_blocks = (n_pairs + N_EXPERTS * (tm - 1) + tm - 1) // tm
    n_rows = n_blocks * tm
    tok = jnp.arange(n_pairs, dtype=jnp.int32) // TOP_K
    row_tok = jnp.zeros((n_rows,), jnp.int32).at[dest].set(tok, unique_indices=True)
    n_used = (pend[-1] // tm).astype(jnp.int32)
    blk = jnp.arange(n_blocks, dtype=jnp.int32)
    start = jnp.minimum(blk, n_used - 1) * tm
    block_e = jnp.sum((pend[None, :] <= start[:, None]).astype(jnp.int32), axis=1)
    block_e = jnp.minimum(block_e, N_EXPERTS - 1).astype(jnp.int32)
    return dest, row_tok, block_e, n_used.reshape(1), n_blocks


def _moe(h, top_i, layer, w_gu, b_gu, w_down, b_down):
    dest, row_tok, block_e, n_used, n_blocks = _moe_plan(top_i)
    xs = _moe_gather(h, row_tok, n_used, n_blocks)
    yb = _moe_experts(xs, block_e, n_used, layer, w_gu, b_gu, w_down, b_down)
    return yb, dest


def _conv_ctx_kernel(u_ref, w_ref, b_ref, o_ref):
    u = u_ref[...]
    n = u.shape[0]
    row = lax.broadcasted_iota(jnp.int32, u.shape, 0)
    w = w_ref[...]
    um2 = jnp.where(row >= 2, pltpu.roll(u, 2, 0), 0.0)
    um1 = jnp.where(row >= 1, pltpu.roll(u, 1, 0), 0.0)
    up1 = jnp.where(row < n - 1, pltpu.roll(u, n - 1, 0), 0.0)
    o_ref[...] = w[0:1] * um2 + w[1:2] * um1 + w[2:3] * u + w[3:4] * up1 + b_ref[...]


def _conv_ctx(z1, conv_w, conv_b, n_x, n_ctx):
    b, l, _ = z1.shape
    w = conv_w.shape[1]
    return pl.pallas_call(
        _conv_ctx_kernel,
        out_shape=jax.ShapeDtypeStruct((b, n_ctx, w), F32),
        grid=(b,),
        in_specs=[
            pl.BlockSpec((None, n_ctx, w), lambda b_: (b_, n_x // n_ctx, 1)),
            pl.BlockSpec((CONV_W, w), lambda b_: (0, 0)),
            pl.BlockSpec((1, w), lambda b_: (0, 0)),
        ],
        out_specs=pl.BlockSpec((None, n_ctx, w), lambda b_: (b_, 0, 0)),
        compiler_params=_params(("parallel",)),
        name="lru_conv_ctx",
    )(z1, conv_w, conv_b.reshape(1, w))


def _conv_x_kernel(u_ref, prev_ref, next_ref, w_ref, b_ref, o_ref):
    g = pl.program_id(1)
    ng = pl.num_programs(1)
    r = u_ref.shape[0]
    wg = u_ref.shape[1]
    w = w_ref[...]
    c0, c1, c2, c3 = w[0:1], w[1:2], w[2:3], w[3:4]
    bias = b_ref[...]
    sub = lax.broadcasted_iota(jnp.int32, u_ref.shape[1:], 0)

    def from_prev_col(x, halo):
        return jnp.where(sub == 0, halo, pltpu.roll(x, 1, 0))

    def from_next_col(x, halo):
        return jnp.where(sub == wg - 1, halo, pltpu.roll(x, wg - 1, 0))

    has_prev = (g > 0).astype(F32)
    has_next = (g < ng - 1).astype(F32)
    hp = prev_ref.shape[0]
    u_m1 = from_prev_col(u_ref[r - 1], prev_ref[hp - 1, wg - 1:wg, :] * has_prev)
    u_m2 = from_prev_col(u_ref[r - 2], prev_ref[hp - 2, wg - 1:wg, :] * has_prev)
    u_p = from_next_col(u_ref[0], next_ref[0, 0:1, :] * has_next)

    def interior(i, carry):
        o_ref[i] = c0 * u_ref[i - 2] + c1 * u_ref[i - 1] + c2 * u_ref[i] + c3 * u_ref[i + 1] + bias
        return carry

    lax.fori_loop(2, r - 1, interior, 0)
    o_ref[0] = c0 * u_m2 + c1 * u_m1 + c2 * u_ref[0] + c3 * u_ref[1] + bias
    o_ref[1] = c0 * u_m1 + c1 * u_ref[0] + c2 * u_ref[1] + c3 * u_ref[2] + bias
    o_ref[r - 1] = c0 * u_ref[r - 3] + c1 * u_ref[r - 2] + c2 * u_ref[r - 1] + c3 * u_p + bias


def _conv_x(z1, conv_w, conv_b, n_x):
    b, l, f = z1.shape
    w = conv_w.shape[1]
    rows = n_x // GRID_W
    wg = W_GROUP
    ng = GRID_W // wg
    z4 = z1.reshape(b, l // GRID_W, GRID_W, f)
    halo = 8
    last_halo = rows // halo - 1
    return pl.pallas_call(
        _conv_x_kernel,
        out_shape=jax.ShapeDtypeStruct((b, rows, GRID_W, w), F32),
        grid=(b, ng),
        in_specs=[
            pl.BlockSpec((None, rows, wg, w), lambda b_, g: (b_, 0, g, 1)),
            pl.BlockSpec((None, halo, wg, w), lambda b_, g: (b_, last_halo, jnp.maximum(g - 1, 0), 1)),
            pl.BlockSpec((None, halo, wg, w), lambda b_, g: (b_, 0, jnp.minimum(g + 1, ng - 1), 1)),
            pl.BlockSpec((CONV_W, w), lambda b_, g: (0, 0)),
            pl.BlockSpec((1, w), lambda b_, g: (0, 0)),
        ],
        out_specs=pl.BlockSpec((None, rows, wg, w), lambda b_, g: (b_, 0, g, 0)),
        compiler_params=_params(("parallel", "parallel")),
        name="lru_conv_x",
    )(z4, z4, z4, conv_w, conv_b.reshape(1, w))


def _gates_kernel(u_ref, wa_ref, wx_ref, ba_ref, bx_ref, lam_ref, a_ref, b_ref, wabf, wxbf):
    @pl.when(pl.program_id(0) == 0)
    def _():
        wabf[...] = wa_ref[...].astype(BF16)
        wxbf[...] = wx_ref[...].astype(BF16)

    for d in range(2):
        lam = lam_ref[d]
        nl = -lam
        softplus = jnp.maximum(nl, 0.0) + jnp.log(1.0 + jnp.exp(-jnp.abs(nl)))
        for h in range(LRU_HEADS):
            sl = slice(h * LRU_BLOCK, (h + 1) * LRU_BLOCK)
            u = u_ref[:, sl]
            ub = u.astype(BF16)
            r = _sigmoid(_dot(ub, wabf[d, h]) + ba_ref[d][:, sl])
            ig = _sigmoid(_dot(ub, wxbf[d, h]) + bx_ref[d][:, sl])
            log_a = (-LRU_C) * r * softplus[:, sl]
            a = jnp.exp(log_a)
            a_ref[d, :, sl] = a
            b_ref[d, :, sl] = jnp.sqrt(1.0 - a * a) * (ig * u)


def _gates(cu, w_a, w_x, b_a, b_x, lam):
    m, w = cu.shape
    tm = ROW_BLK
    full5 = pl.BlockSpec(w_a.shape, lambda i: (0, 0, 0, 0))
    vec = pl.BlockSpec((2, 1, w), lambda i: (0, 0, 0))
    out = pl.BlockSpec((2, tm, w), lambda i: (0, i, 0))
    return pl.pallas_call(
        _gates_kernel,
        out_shape=(jax.ShapeDtypeStruct((2, m, w), F32), jax.ShapeDtypeStruct((2, m, w), F32)),
        grid=(m // tm,),
        in_specs=[pl.BlockSpec((tm, w), lambda i: (i, 0)), full5, full5, vec, vec, vec],
        out_specs=(out, out),
        scratch_shapes=[pltpu.VMEM(w_a.shape, BF16), pltpu.VMEM(w_a.shape, BF16)],
        compiler_params=_params(("arbitrary",)),
        name="lru_gates",
    )(cu, w_a, w_x, b_a.reshape(2, 1, w), b_x.reshape(2, 1, w), lam.reshape(2, 1, w))


def _lru_scan_kernel(ax_ref, bx_ref, ac_ref, bc_ref, h_ref, carry, acum, *, rev):
    s = pl.program_id(2)
    rows = ax_ref.shape[0]
    wg = ax_ref.shape[1]
    n_ctx = ac_ref.shape[0]
    ft = ax_ref.shape[2]

    @pl.when(s == 0)
    def _():
        def body(t, h):
            tt = n_ctx - 1 - t if rev else t
            return ac_ref[pl.ds(tt, 1), :] * h + bc_ref[pl.ds(tt, 1), :]

        carry[...] = lax.fori_loop(0, n_ctx, body, jnp.zeros((1, ft), F32))

    @pl.when(s > 0)
    def _():
        def seg(i, c):
            a_run, h_run = c
            r = rows - 1 - i if rev else i
            a = ax_ref[r]
            h_run = a * h_run + bx_ref[r]
            a_run = a_run * a
            h_ref[r] = h_run
            acum[r] = a_run
            return a_run, h_run

        a_tot, h_tot = lax.fori_loop(
            0, rows, seg, (jnp.ones((wg, ft), F32), jnp.zeros((wg, ft), F32)))
        h = carry[...]
        sub = lax.broadcasted_iota(jnp.int32, (wg, ft), 0)
        h_in = jnp.zeros((wg, ft), F32)
        for j in (range(wg - 1, -1, -1) if rev else range(wg)):
            h_in = jnp.where(sub == j, h, h_in)
            h = a_tot[j:j + 1] * h + h_tot[j:j + 1]
        carry[...] = h

        def fix(r, c):
            h_ref[r] = h_ref[r] + acum[r] * h_in
            return c

        lax.fori_loop(0, rows, fix, 0)


def _lru_scan(a_x, b_x, a_c, b_c, d, rev):
    _, b, rows, gw, w = a_x.shape
    n_ctx = a_c.shape[2]
    wg = W_GROUP
    ng = gw // wg
    ft = 1024

    def grp(s):
        k = jnp.maximum(s - 1, 0)
        return ng - 1 - k if rev else k

    xblk = pl.BlockSpec((None, None, rows, wg, ft), lambda b_, f, s: (d, b_, 0, grp(s), f))
    cblk = pl.BlockSpec((None, None, n_ctx, ft), lambda b_, f, s: (d, b_, 0, f))
    return pl.pallas_call(
        functools.partial(_lru_scan_kernel, rev=rev),
        out_shape=jax.ShapeDtypeStruct((b, rows, gw, w), F32),
        grid=(b, w // ft, ng + 1),
        in_specs=[xblk, xblk, cblk, cblk],
        out_specs=pl.BlockSpec((None, rows, wg, ft), lambda b_, f, s: (b_, 0, grp(s), f)),
        scratch_shapes=[pltpu.VMEM((1, ft), F32), pltpu.VMEM((rows, wg, ft), F32)],
        compiler_params=_params(("parallel", "parallel", "arbitrary")),
        name="lru_scan_bwd" if rev else "lru_scan_fwd",
    )(a_x, b_x, a_c, b_c)


def _lru_prep_kernel(y_ref, hf_ref, hb_ref, o_ref):
    o_ref[...] = (y_ref[...] * (hf_ref[...] + hb_ref[...])).astype(o_ref.dtype)


def _lru_prep(z1, h_f, h_b, n_x):
    b, l, _ = z1.shape
    w = h_f.shape[-1]
    blk = pl.BlockSpec((None, ROW_BLK, w), lambda b_, t: (b_, t, 0))
    return pl.pallas_call(
        _lru_prep_kernel,
        out_shape=jax.ShapeDtypeStruct((b, n_x, w), BF16),
        grid=(b, n_x // ROW_BLK),
        in_specs=[blk, blk, blk],
        out_specs=blk,
        compiler_params=_params(("parallel", "parallel")),
        name="lru_prep",
    )(z1, h_f, h_b)


def kernel(x, c, ctx, c_ctx, ada_w, ada_b, ln1_g, ln1_b, ln2_g, ln2_b, hg_w_in, hg_lb_logits, hg_norm_g, hg_w_out, lru_w_in, lru_conv_w, lru_conv_b, lru_w_a, lru_b_a, lru_w_x, lru_b_x, lru_lam, lru_w_out, moe_w_router, moe_b_router, moe_w_gu, moe_b_gu, moe_w_down, moe_b_down):
    bsz, n_x, d = x.shape
    n_ctx = ctx.shape[1]
    assert d == D_MODEL and ada_w.shape[0] == DEPTH == 2
    assert n_x % ROW_BLK == 0 and n_ctx == ROW_BLK and n_x % (GRID_W * 8) == 0 and bsz < 8
    l = n_x + n_ctx
    n_x_blocks = n_x // ROW_BLK
    n_blocks = l // ROW_BLK
    ctx_row = bsz

    cv = jnp.zeros((8, d), F32).at[:bsz].set(c).at[bsz].set(c_ctx)
    mod = _ada(cv, ada_w, ada_b).reshape(DEPTH, 8, 6, 1, d)
    xcat = jnp.concatenate([x, ctx], axis=1)

    h0 = _modulate(xcat, mod[0], n_x_blocks, ctx_row)
    lb = jnp.cumsum(jax.nn.softmax(hg_lb_logits.astype(F32), axis=1), axis=1)[:, 0]
    lb_row = jnp.zeros((1, 5 * d), F32).at[0, d:3 * d].set(lb.reshape(-1))
    z = _matmul(h0.reshape(bsz * l, d), hg_w_in[0], mode="hgrn_in", lb_row=lb_row, section=d)
    o2 = _hgrn_scan(z.reshape(bsz, l, 5 * d), n_x)
    a0 = _hgrn_prep(o2.reshape(2, bsz * l, d), z, hg_norm_g[0])
    y0 = _matmul(a0, hg_w_out[0]).reshape(bsz, l, d)
    x1, h1, ti, tg = _post(xcat, y0, mod[0], ln1_g[0], ln1_b[0], moe_w_router[0], moe_b_router[0],
                           n_blocks, n_x_blocks, ctx_row)
    yb, dest = _moe(h1.reshape(bsz * l, d), ti.reshape(bsz * l, 128), 0,
                    moe_w_gu, moe_b_gu, moe_w_down, moe_b_down)
    x2, h2 = _combine(x1, yb, dest, tg, mod[0], ln2_g[0], ln2_b[0], n_x_blocks, ctx_row, mod_next=mod[1])

    z1 = _matmul(h2.reshape(bsz * l, d), lru_w_in[0], mode="lru_in", section=d).reshape(bsz, l, 2 * d)
    cu_x = _conv_x(z1, lru_conv_w[0], lru_conv_b[0], n_x)
    cu_c = _conv_ctx(z1, lru_conv_w[0], lru_conv_b[0], n_x, n_ctx)
    rows = n_x // GRID_W
    a_x, b_x = _gates(cu_x.reshape(bsz * n_x, d), lru_w_a[0], lru_w_x[0], lru_b_a[0], lru_b_x[0], lru_lam[0])
    a_c, b_c = _gates(cu_c.reshape(bsz * n_ctx, d), lru_w_a[0], lru_w_x[0], lru_b_a[0], lru_b_x[0], lru_lam[0])
    a_x = a_x.reshape(2, bsz, rows, GRID_W, d)
    b_x = b_x.reshape(2, bsz, rows, GRID_W, d)
    a_c = a_c.reshape(2, bsz, n_ctx, d)
    b_c = b_c.reshape(2, bsz, n_ctx, d)
    h_f = _lru_scan(a_x, b_x, a_c, b_c, 0, rev=False).reshape(bsz, n_x, d)
    h_b = _lru_scan(a_x, b_x, a_c, b_c, 1, rev=True).reshape(bsz, n_x, d)
    a1 = _lru_prep(z1, h_f, h_b, n_x)
    y1 = _matmul(a1.reshape(bsz * n_x, d), lru_w_out[0]).reshape(bsz, n_x, d)
    x3, h3, ti, tg = _post(x2, y1, mod[1], ln1_g[1], ln1_b[1], moe_w_router[1], moe_b_router[1],
                           n_x_blocks, n_x_blocks, ctx_row)
    yb, dest = _moe(h3.reshape(bsz * n_x, d), ti.reshape(bsz * n_x, 128), 1,
                    moe_w_gu, moe_b_gu, moe_w_down, moe_b_down)
    (out,) = _combine(x3, yb, dest, tg, mod[1], ln2_g[1], ln2_b[1], n_x_blocks, ctx_row)
    return out
```

```python
import functools

import numpy as np
import jax
import jax.numpy as jnp
from jax import lax
from jax.experimental import pallas as pl
from jax.experimental.pallas import tpu as pltpu

F32 = jnp.float32
BF16 = jnp.bfloat16

D_MODEL = 2048
DEPTH = 2
GRID_W = 64
HG_HEADS = 16
HG_DK = 128
LRU_HEADS = 8
LRU_BLOCK = D_MODEL // LRU_HEADS
CONV_W = 4
LRU_C = 8.0
N_EXPERTS = 32
TOP_K = 4
SWIGLU_LIMIT = 7.0
SWIGLU_ALPHA = 1.702
DN_ALPHA = float((2 * DEPTH) ** 0.25)
LN_EPS = 1e-5
RMS_EPS = 1e-6
ROUTER_LANES = 128
ROUTER_PAD = -1e30

ROW_BLK = 256
MM_TM = 1024
MM_TN = 1024
MM_SUB = 256
HG_CHUNK = 128
HG_GROUP = 16
MOE_TM = 512
MOE_TN = 1024
CMB_TM = 128
DMA_UNROLL = 8
W_GROUP = 8
VMEM_LIMIT = 56 * 1024 * 1024


def _params(sem):
    return pltpu.CompilerParams(dimension_semantics=sem, vmem_limit_bytes=VMEM_LIMIT)


def _split_bf16(a):
    hi = a.astype(BF16)
    lo = (a - hi.astype(F32)).astype(BF16)
    return hi, lo


def _dot(a, b):
    return jnp.dot(a, b, preferred_element_type=F32)


def _dot3(a, b):
    ah, al = _split_bf16(a)
    bh, bl = _split_bf16(b)
    return _dot(ah, bh) + _dot(ah, bl) + _dot(al, bh)


def _dot_nt(a, b):
    return lax.dot_general(a, b, (((1,), (1,)), ((), ())), preferred_element_type=F32)


def _dot_tn(a, b):
    return lax.dot_general(a, b, (((0,), (0,)), ((), ())), preferred_element_type=F32)


def _sigmoid(x):
    return 0.5 * (1.0 + jnp.tanh(0.5 * x))


def _ada_kernel(cv_ref, w_ref, b_ref, o_ref):
    x = cv_ref[...]
    s = x * _sigmoid(x)
    o_ref[...] = _dot3(s, w_ref[...]) + b_ref[...]


def _ada(cv, ada_w, ada_b):
    depth, d, n = ada_w.shape
    tn = 1024
    return pl.pallas_call(
        _ada_kernel,
        out_shape=jax.ShapeDtypeStruct((depth, 8, n), F32),
        grid=(depth, n // tn),
        in_specs=[
            pl.BlockSpec((8, d), lambda l, j: (0, 0)),
            pl.BlockSpec((None, d, tn), lambda l, j: (l, 0, j)),
            pl.BlockSpec((None, 1, tn), lambda l, j: (l, 0, j)),
        ],
        out_specs=pl.BlockSpec((None, 8, tn), lambda l, j: (l, 0, j)),
        compiler_params=_params(("arbitrary", "arbitrary")),
        name="ada_mod",
    )(cv, ada_w, ada_b.reshape(depth, 1, n))


def _mod_spec(section, n_x_blocks, ctx_row):
    def index(b, t):
        return (jnp.where(t < n_x_blocks, b, ctx_row), section, 0, 0)
    return pl.BlockSpec((None, None, 1, D_MODEL), index)


def _modulate_kernel(x_ref, sh_ref, sc_ref, o_ref):
    o_ref[...] = (x_ref[...] * (1.0 + sc_ref[...]) + sh_ref[...]).astype(o_ref.dtype)


def _modulate(xcat, mod, n_x_blocks, ctx_row):
    b, l, d = xcat.shape
    blk = pl.BlockSpec((None, ROW_BLK, d), lambda b_, t: (b_, t, 0))
    return pl.pallas_call(
        _modulate_kernel,
        out_shape=jax.ShapeDtypeStruct((b, l, d), BF16),
        grid=(b, l // ROW_BLK),
        in_specs=[blk, _mod_spec(0, n_x_blocks, ctx_row), _mod_spec(1, n_x_blocks, ctx_row)],
        out_specs=blk,
        compiler_params=_params(("parallel", "parallel")),
        name="modulate",
    )(xcat, mod, mod)


def _gelu_tanh(x):
    return 0.5 * x * (1.0 + jnp.tanh(0.7978845608028654 * (x + 0.044715 * (x * x * x))))


def _mm_kernel(a_ref, w_ref, *rest, mode, tiles_per_section):
    o_ref, wbf = rest[-2], rest[-1]

    @pl.when(pl.program_id(1) == 0)
    def _():
        wbf[...] = w_ref[...].astype(BF16)

    sec = pl.program_id(0) // tiles_per_section
    tn = o_ref.shape[1]

    def tile(epilogue):
        for c in range(tn // MM_SUB):
            cs = slice(c * MM_SUB, (c + 1) * MM_SUB)
            o_ref[:, cs] = epilogue(_dot(a_ref[...], wbf[:, cs]), cs)

    def identity(acc, cs):
        return acc

    if mode == "plain":
        tile(identity)
    elif mode == "hgrn_in":
        lb_ref = rest[0]

        @pl.when((sec == 0) | (sec == 4))
        def _():
            tile(lambda acc, cs: acc * _sigmoid(acc))

        @pl.when((sec == 1) | (sec == 2))
        def _():
            tile(lambda acc, cs: lb_ref[:, cs] + (1.0 - lb_ref[:, cs]) * _sigmoid(acc))

        @pl.when(sec == 3)
        def _():
            tile(identity)
    elif mode == "lru_in":
        @pl.when(sec == 0)
        def _():
            tile(lambda acc, cs: _gelu_tanh(acc))

        @pl.when(sec == 1)
        def _():
            tile(identity)
    else:
        raise ValueError(mode)


def _matmul(a, w, mode="plain", lb_row=None, section=None):
    m, k = a.shape
    n = w.shape[1]
    tm = MM_TM if m % MM_TM == 0 else ROW_BLK
    tn = MM_TN
    section = section or n
    in_specs = [
        pl.BlockSpec((tm, k), lambda j, i: (i, 0)),
        pl.BlockSpec((k, tn), lambda j, i: (0, j)),
    ]
    args = [a, w]
    if lb_row is not None:
        in_specs.append(pl.BlockSpec((1, tn), lambda j, i: (0, j)))
        args.append(lb_row)
    return pl.pallas_call(
        functools.partial(_mm_kernel, mode=mode, tiles_per_section=section // tn),
        out_shape=jax.ShapeDtypeStruct((m, n), F32),
        grid=(n // tn, m // tm),
        in_specs=in_specs,
        out_specs=pl.BlockSpec((tm, tn), lambda j, i: (i, j)),
        scratch_shapes=[pltpu.VMEM((k, tn), BF16)],
        compiler_params=_params(("arbitrary", "arbitrary")),
        name="matmul_" + mode,
    )(*args)


def _hgrn_tables(c):
    levels = []
    bs = c
    while bs >= 2:
        levels.append(bs)
        bs //= 2
    nlev = len(levels)
    g = np.zeros((2, (2 + nlev) * c, c), np.float32)
    masks = np.zeros((2, nlev + 1, c, c), np.float32)
    r = np.arange(c)
    for d in range(2):
        for row in range(c):
            if d == 0:
                g[d, row, : row + 1] = 1.0
                g[d, c + row, row + 1:] = 1.0
            else:
                g[d, row, row:] = 1.0
                g[d, c + row, :row] = 1.0
        for li, bs in enumerate(levels):
            half = bs // 2
            for row in range(c):
                base = (row // bs) * bs
                mid = base + half
                pos = row - base
                blk = g[d, (2 + li) * c + row]
                if d == 0:
                    if pos >= half:
                        blk[mid: row + 1] = 1.0
                    else:
                        blk[row + 1: mid] = 1.0
                else:
                    if pos < half:
                        blk[row: mid] = 1.0
                    else:
                        blk[mid: row] = 1.0
            same = (r[:, None] // bs) == (r[None, :] // bs)
            t_hi = (r[:, None] % bs) >= half
            s_hi = (r[None, :] % bs) >= half
            if d == 0:
                masks[d, li] = same & t_hi & ~s_hi
            else:
                masks[d, li] = same & ~t_hi & s_hi
        masks[d, nlev] = np.eye(c)
    return jnp.asarray(g, BF16), jnp.asarray(masks, F32), nlev


def _hgrn_scan_kernel(q_ref, f_ref, v_ref, g_ref, m_ref, o_ref, st_ref, *, nlev):
    c = q_ref.shape[0]

    @pl.when(pl.program_id(3) == 0)
    def _():
        st_ref[...] = jnp.zeros_like(st_ref)

    gmat = g_ref[...]
    fwd = pl.program_id(1) == 0
    nbig = nlev - (gmat.shape[0] // c - 1)

    for h in range(HG_GROUP):
        sl = slice(h * HG_DK, (h + 1) * HG_DK)
        f = f_ref[:, sl]
        q = q_ref[:, sl]
        vb = v_ref[:, sl].astype(BF16)
        lf = jnp.log(f)
        k = 1.0 - f
        lf_hi, lf_lo = _split_bf16(lf)
        ex2 = _dot(gmat, jnp.concatenate([lf_hi, lf_lo], axis=1))
        ex = ex2[:, :HG_DK] + ex2[:, HG_DK:]
        cum = ex[0:c]
        total = jnp.where(fwd, cum[c - 1:c], cum[0:1])
        e_lev = []
        bs = c
        for li in range(nbig):
            half = bs // 2
            pieces = []
            for base in range(0, c, bs):
                ref = jnp.where(fwd, cum[base + half - 1:base + half], cum[base + half:base + half + 1])
                pieces.append(-jnp.abs(cum[base:base + bs] - ref))
            e_lev.append(jnp.exp(pieces[0] if len(pieces) == 1 else jnp.concatenate(pieces, axis=0)))
            bs = half
        for li in range(nlev - nbig):
            e_lev.append(jnp.exp(ex[(1 + li) * c:(2 + li) * c]))
        st = st_ref[h]
        o = _dot_nt((q * jnp.exp(cum)).astype(BF16), st.astype(BF16))
        scores = m_ref[nlev] * _dot_nt(q.astype(BF16), k.astype(BF16))
        for li in range(nlev):
            el = e_lev[li]
            scores = scores + m_ref[li] * _dot_nt((q * el).astype(BF16), (k * el).astype(BF16))
        o = o + _dot(scores.astype(BF16), vb)
        o_ref[:, sl] = o
        st_ref[h] = st * jnp.exp(total) + _dot_tn(vb, (k * jnp.exp(total - cum)).astype(BF16))


def _hgrn_scan(z, n_x):
    b, l, _ = z.shape
    c = HG_CHUNK
    nc = l // c
    ncx = n_x // c
    gmat, masks, nlev = _hgrn_tables(c)
    nbig = nlev - 3
    gmat = jnp.concatenate([gmat[:, :c], gmat[:, (2 + nbig) * c:]], axis=1)

    def chunk(d, s):
        fwd = jnp.where(s < nc - ncx, ncx + s, s - (nc - ncx))
        return jnp.where(d == 0, fwd, nc - 1 - s)

    gw = HG_GROUP * HG_DK
    ngrp = D_MODEL // gw

    def spec(section):
        return pl.BlockSpec((None, c, gw), lambda b_, d, g, s: (b_, chunk(d, s), section(d) * ngrp + g))

    return pl.pallas_call(
        functools.partial(_hgrn_scan_kernel, nlev=nlev),
        out_shape=jax.ShapeDtypeStruct((2, b, l, D_MODEL), F32),
        grid=(b, 2, ngrp, nc),
        in_specs=[
            spec(lambda d: 0),
            spec(lambda d: 1 + d),
            spec(lambda d: 3),
            pl.BlockSpec((None,) + gmat.shape[1:], lambda b_, d, g, s: (d, 0, 0)),
            pl.BlockSpec((None,) + masks.shape[1:], lambda b_, d, g, s: (d, 0, 0, 0)),
        ],
        out_specs=pl.BlockSpec((None, None, c, gw), lambda b_, d, g, s: (d, b_, chunk(d, s), g)),
        scratch_shapes=[pltpu.VMEM((HG_GROUP, HG_DK, HG_DK), F32)],
        compiler_params=_params(("parallel", "arbitrary", "arbitrary", "arbitrary")),
        name="hgrn_scan",
    )(z, z, z, gmat, masks)


def _hgrn_prep_kernel(of_ref, ob_ref, g_ref, ng_ref, o_ref):
    ng = ng_ref[...]
    for h in range(HG_HEADS):
        sl = slice(h * HG_DK, (h + 1) * HG_DK)
        o = of_ref[:, sl] + ob_ref[:, sl]
        ms = jnp.mean(o * o, axis=-1, keepdims=True)
        y = o * lax.rsqrt(ms + RMS_EPS) * ng
        o_ref[:, sl] = (y * g_ref[:, sl]).astype(o_ref.dtype)


def _hgrn_prep(o2, z, norm_g):
    _, m, d = o2.shape
    tm = 512 if m % 512 == 0 else ROW_BLK
    return pl.pallas_call(
        _hgrn_prep_kernel,
        out_shape=jax.ShapeDtypeStruct((m, d), BF16),
        grid=(m // tm,),
        in_specs=[
            pl.BlockSpec((None, tm, d), lambda i: (0, i, 0)),
            pl.BlockSpec((None, tm, d), lambda i: (1, i, 0)),
            pl.BlockSpec((tm, d), lambda i: (i, 4)),
            pl.BlockSpec((1, HG_DK), lambda i: (0, 0)),
        ],
        out_specs=pl.BlockSpec((tm, d), lambda i: (i, 0)),
        compiler_params=_params(("parallel",)),
        name="hgrn_prep",
    )(o2, o2, z, norm_g.reshape(1, HG_DK))


def _layer_norm(v, g, b):
    mu = jnp.mean(v, axis=-1, keepdims=True)
    vc = v - mu
    var = jnp.mean(vc * vc, axis=-1, keepdims=True)
    return vc * lax.rsqrt(var + LN_EPS) * g + b


def _router_topk(h, wr, br, ti_ref, tg_ref):
    logits = _dot3(h, wr) + br
    lane = lax.broadcasted_iota(jnp.int32, logits.shape, 1)
    lane_f = lane.astype(F32)
    vals, ids = [], []
    cur = logits
    for _ in range(TOP_K):
        mx = jnp.max(cur, axis=-1, keepdims=True)
        idx = jnp.min(jnp.where(cur == mx, lane_f, float(ROUTER_LANES)), axis=-1, keepdims=True)
        vals.append(mx)
        ids.append(idx)
        cur = jnp.where(lane_f == idx, -jnp.inf, cur)
    ex = [jnp.exp(v - vals[0]) for v in vals]
    den = ex[0] + ex[1] + ex[2] + ex[3]
    ti = jnp.zeros(logits.shape, F32)
    tg = jnp.zeros(logits.shape, F32)
    for kk in range(TOP_K):
        ti = jnp.where(lane == kk, ids[kk], ti)
        tg = jnp.where(lane == kk, ex[kk] / den, tg)
    ti_ref[...] = ti.astype(jnp.int32)
    tg_ref[...] = tg


def _post_kernel(x_ref, y_ref, gate_ref, lng_ref, lnb_ref, sh_ref, sc_ref, wr_ref, br_ref,
                 xo_ref, h_ref, ti_ref, tg_ref):
    v = DN_ALPHA * x_ref[...] + gate_ref[...] * y_ref[...]
    xn = _layer_norm(v, lng_ref[...], lnb_ref[...])
    xo_ref[...] = xn
    h = xn * (1.0 + sc_ref[...]) + sh_ref[...]
    h_ref[...] = h
    _router_topk(h, wr_ref[...], br_ref[...], ti_ref, tg_ref)


def _post(x, y, mod, ln_g, ln_b, w_router, b_router, n_blocks, n_x_blocks, ctx_row):
    b, _, d = x.shape
    rows = n_blocks * ROW_BLK
    blk = pl.BlockSpec((None, ROW_BLK, d), lambda b_, t: (b_, t, 0))
    lane_blk = pl.BlockSpec((None, ROW_BLK, 128), lambda b_, t: (b_, t, 0))
    row = pl.BlockSpec((1, d), lambda b_, t: (0, 0))
    ms = functools.partial(_mod_spec, n_x_blocks=n_x_blocks, ctx_row=ctx_row)
    wr = jnp.zeros((d, ROUTER_LANES), F32).at[:, :N_EXPERTS].set(w_router)
    br = jnp.full((1, ROUTER_LANES), ROUTER_PAD, F32).at[0, :N_EXPERTS].set(b_router)
    return pl.pallas_call(
        _post_kernel,
        out_shape=(
            jax.ShapeDtypeStruct((b, rows, d), F32),
            jax.ShapeDtypeStruct((b, rows, d), F32),
            jax.ShapeDtypeStruct((b, rows, 128), jnp.int32),
            jax.ShapeDtypeStruct((b, rows, 128), F32),
        ),
        grid=(b, n_blocks),
        in_specs=[blk, blk, ms(2), row, row, ms(3), ms(4),
                  pl.BlockSpec((d, ROUTER_LANES), lambda b_, t: (0, 0)),
                  pl.BlockSpec((1, ROUTER_LANES), lambda b_, t: (0, 0))],
        out_specs=(blk, blk, lane_blk, lane_blk),
        compiler_params=_params(("parallel", "parallel")),
        name="post_mixer",
    )(x, y, mod, ln_g.reshape(1, d), ln_b.reshape(1, d), mod, mod, wr, br)


def _gather_kernel(nu_ref, idx_ref, h_hbm, o_ref, buf, sem):
    i = pl.program_id(0)
    tm = o_ref.shape[0]
    nu = nu_ref[0]

    def issue(block, slot):
        base = block * tm

        def start(r8, carry):
            for u in range(DMA_UNROLL):
                r = r8 * DMA_UNROLL + u
                tok = idx_ref[base + r]
                pltpu.make_async_copy(
                    h_hbm.at[pl.ds(tok, 1), :], buf.at[slot, pl.ds(r, 1), :], sem.at[slot]
                ).start(priority=u % 2)
            return carry

        lax.fori_loop(0, tm // DMA_UNROLL, start, 0)

    @pl.when(i == 0)
    def _():
        issue(0, 0)

    @pl.when(i + 1 < nu)
    def _():
        issue(i + 1, (i + 1) % 2)

    @pl.when(i < nu)
    def _():
        slot = i % 2
        pltpu.make_async_copy(buf.at[slot], buf.at[slot], sem.at[slot]).wait()
        o_ref[...] = buf[slot].astype(o_ref.dtype)

    @pl.when(i >= nu)
    def _():
        o_ref[...] = jnp.zeros_like(o_ref)


def _moe_gather(h, row_tok, n_used, n_blocks):
    t, d = h.shape
    tm = MOE_TM
    return pl.pallas_call(
        _gather_kernel,
        out_shape=jax.ShapeDtypeStruct((n_blocks * tm, d), BF16),
        grid_spec=pltpu.PrefetchScalarGridSpec(
            num_scalar_prefetch=2,
            grid=(n_blocks,),
            in_specs=[pl.BlockSpec(memory_space=pl.ANY)],
            out_specs=pl.BlockSpec((tm, d), lambda i, nu, idx: (i, 0)),
            scratch_shapes=[pltpu.VMEM((2, tm, d), F32), pltpu.SemaphoreType.DMA((2,))],
        ),
        compiler_params=_params(("arbitrary",)),
        name="moe_gather",
    )(n_used, row_tok, h)


def _expert_changed(be_ref, i):
    return (i == 0) | (be_ref[i] != be_ref[jnp.maximum(i - 1, 0)])


def _moe_gu_kernel(be_ref, nu_ref, x_ref, w_ref, b_ref, perm_ref, o_ref, wbf):
    i = pl.program_id(1)
    tn = w_ref.shape[1]
    sub = perm_ref.shape[0]
    half = sub // 2

    @pl.when(i < nu_ref[0])
    def _():
        @pl.when(_expert_changed(be_ref, i))
        def _():
            for c in range(tn // sub):
                cs = slice(c * sub, (c + 1) * sub)
                wbf[:, cs] = _dot(w_ref[:, cs].astype(BF16), perm_ref[...]).astype(BF16)

        for c in range(tn // sub):
            cs = slice(c * sub, (c + 1) * sub)
            gu = _dot(x_ref[...], wbf[:, cs]) + b_ref[:, cs]
            g = jnp.minimum(gu[:, :half], SWIGLU_LIMIT)
            u = jnp.clip(gu[:, half:], -SWIGLU_LIMIT, SWIGLU_LIMIT)
            act = (u + 1.0) * (g * _sigmoid(SWIGLU_ALPHA * g))
            o_ref[:, c * half:(c + 1) * half] = act.astype(o_ref.dtype)

    @pl.when(i >= nu_ref[0])
    def _():
        o_ref[...] = jnp.zeros_like(o_ref)


def _moe_down_kernel(be_ref, nu_ref, a_ref, w_ref, b_ref, o_ref, wbf):
    i = pl.program_id(1)

    @pl.when(i < nu_ref[0])
    def _():
        @pl.when(_expert_changed(be_ref, i))
        def _():
            wbf[...] = w_ref[...].astype(BF16)

        o_ref[...] = _dot(a_ref[...], wbf[...]) + b_ref[...]

    @pl.when(i >= nu_ref[0])
    def _():
        o_ref[...] = jnp.zeros_like(o_ref)


def _moe_experts(xs, block_e, n_used, layer, w_gu, b_gu, w_down, b_down):
    n_rows, d = xs.shape
    tm, tn = MOE_TM, MOE_TN
    n_blocks = n_rows // tm
    de2 = w_gu.shape[-1]
    de = de2 // 2
    sub = MM_SUB
    half = sub // 2
    perm = np.zeros((sub, sub), np.float32)
    perm[2 * np.arange(half), np.arange(half)] = 1.0
    perm[2 * np.arange(half) + 1, half + np.arange(half)] = 1.0
    perm = jnp.asarray(perm, BF16)
    b_perm = b_gu.reshape(DEPTH, N_EXPERTS, de2 // sub, half, 2)
    b_perm = jnp.swapaxes(b_perm, -1, -2).reshape(DEPTH, N_EXPERTS, 1, de2)

    def row(i, nu):
        return jnp.minimum(i, nu[0] - 1)

    act = pl.pallas_call(
        _moe_gu_kernel,
        out_shape=jax.ShapeDtypeStruct((n_rows, de), BF16),
        grid_spec=pltpu.PrefetchScalarGridSpec(
            num_scalar_prefetch=2,
            grid=(de2 // tn, n_blocks),
            in_specs=[
                pl.BlockSpec((tm, d), lambda j, i, be, nu: (row(i, nu), 0)),
                pl.BlockSpec((None, None, d, tn), lambda j, i, be, nu: (layer, be[i], 0, j)),
                pl.BlockSpec((None, None, 1, tn), lambda j, i, be, nu: (layer, be[i], 0, j)),
                pl.BlockSpec((sub, sub), lambda j, i, be, nu: (0, 0)),
            ],
            out_specs=pl.BlockSpec((tm, tn // 2), lambda j, i, be, nu: (i, j)),
            scratch_shapes=[pltpu.VMEM((d, tn), BF16)],
        ),
        compiler_params=_params(("arbitrary", "arbitrary")),
        name="moe_gate_up",
    )(block_e, n_used, xs, w_gu, b_perm, perm)

    yb = pl.pallas_call(
        _moe_down_kernel,
        out_shape=jax.ShapeDtypeStruct((n_rows, d), F32),
        grid_spec=pltpu.PrefetchScalarGridSpec(
            num_scalar_prefetch=2,
            grid=(d // tn, n_blocks),
            in_specs=[
                pl.BlockSpec((tm, de), lambda j, i, be, nu: (row(i, nu), 0)),
                pl.BlockSpec((None, None, de, tn), lambda j, i, be, nu: (layer, be[i], 0, j)),
                pl.BlockSpec((None, None, 1, tn), lambda j, i, be, nu: (layer, be[i], 0, j)),
            ],
            out_specs=pl.BlockSpec((tm, tn), lambda j, i, be, nu: (i, j)),
            scratch_shapes=[pltpu.VMEM((de, tn), BF16)],
        ),
        compiler_params=_params(("arbitrary", "arbitrary")),
        name="moe_down",
    )(block_e, n_used, act, w_down, b_down.reshape(DEPTH, N_EXPERTS, 1, d))
    return yb


def _combine_kernel(dest_ref, tg_ref, x_ref, gate_ref, lng_ref, lnb_ref, *rest, with_next):
    if with_next:
        sh_ref, sc_ref, yb_hbm, xo_ref, h_ref, buf, sem = rest
    else:
        yb_hbm, xo_ref, buf, sem = rest
    tmc = x_ref.shape[0]
    step = pl.program_id(0) * pl.num_programs(1) + pl.program_id(1)
    n_steps = pl.num_programs(0) * pl.num_programs(1)

    def issue(stp, slot):
        base = stp * (tmc * TOP_K)
        per_iter = DMA_UNROLL // TOP_K

        def start(t2, carry):
            for u in range(per_iter):
                t = t2 * per_iter + u
                for kk in range(TOP_K):
                    row = dest_ref[base + t * TOP_K + kk]
                    pltpu.make_async_copy(
                        yb_hbm.at[pl.ds(row, 1), :], buf.at[slot, kk, pl.ds(t, 1), :], sem.at[slot]
                    ).start(priority=kk % 2)
            return carry

        lax.fori_loop(0, tmc // per_iter, start, 0)

    @pl.when(step == 0)
    def _():
        issue(0, 0)

    @pl.when(step + 1 < n_steps)
    def _():
        issue(step + 1, (step + 1) % 2)

    slot = step % 2
    pltpu.make_async_copy(buf.at[slot], buf.at[slot], sem.at[slot]).wait()

    tg = tg_ref[...]
    y = tg[:, 0:1] * buf[slot, 0]
    for kk in range(1, TOP_K):
        y = y + tg[:, kk:kk + 1] * buf[slot, kk]
    v = DN_ALPHA * x_ref[...] + gate_ref[...] * y
    xn = _layer_norm(v, lng_ref[...], lnb_ref[...])
    xo_ref[...] = xn
    if with_next:
        h_ref[...] = (xn * (1.0 + sc_ref[...]) + sh_ref[...]).astype(h_ref.dtype)


def _combine(x, yb, dest, tg, mod, ln_g, ln_b, n_x_blocks, ctx_row, mod_next=None):
    b, l, d = x.shape
    tmc = CMB_TM
    per = ROW_BLK // tmc
    with_next = mod_next is not None

    def mod_spec(section):
        def index(b_, t, dest_):
            return (jnp.where(t // per < n_x_blocks, b_, ctx_row), section, 0, 0)
        return pl.BlockSpec((None, None, 1, d), index)

    blk = pl.BlockSpec((None, tmc, d), lambda b_, t, dest_: (b_, t, 0))
    row = pl.BlockSpec((1, d), lambda b_, t, dest_: (0, 0))
    in_specs = [pl.BlockSpec((None, tmc, 128), lambda b_, t, dest_: (b_, t, 0)), blk, mod_spec(5), row, row]
    args = [tg, x, mod, ln_g.reshape(1, d), ln_b.reshape(1, d)]
    out_shape = [jax.ShapeDtypeStruct((b, l, d), F32)]
    out_specs = [blk]
    if with_next:
        in_specs += [mod_spec(0), mod_spec(1)]
        args += [mod_next, mod_next]
        out_shape.append(jax.ShapeDtypeStruct((b, l, d), BF16))
        out_specs.append(blk)
    in_specs.append(pl.BlockSpec(memory_space=pl.ANY))
    args.append(yb)
    return pl.pallas_call(
        functools.partial(_combine_kernel, with_next=with_next),
        out_shape=tuple(out_shape),
        grid_spec=pltpu.PrefetchScalarGridSpec(
            num_scalar_prefetch=1,
            grid=(b, l // tmc),
            in_specs=in_specs,
            out_specs=tuple(out_specs),
            scratch_shapes=[pltpu.VMEM((2, TOP_K, tmc, d), F32), pltpu.SemaphoreType.DMA((2,))],
        ),
        compiler_params=_params(("arbitrary", "arbitrary")),
        name="moe_combine",
    )(dest, *args)


def _rank_kernel(ti_ref, tri_ref, rank_ref, cnt_ref, run):
    @pl.when(pl.program_id(0) == 0)
    def _():
        run[...] = jnp.zeros_like(run)

    ti = ti_ref[...]
    lane = lax.broadcasted_iota(jnp.int32, ti.shape, 1)
    onehot = [(lane == ti[:, k:k + 1]).astype(F32) for k in range(TOP_K)]
    per_token = onehot[0]
    for k in range(1, TOP_K):
        per_token = per_token + onehot[k]
    before = _dot(tri_ref[...], per_token.astype(BF16)) + run[...]
    rank = jnp.zeros(ti.shape, F32)
    for k in range(TOP_K):
        rank = jnp.where(lane == k, jnp.sum(onehot[k] * before, axis=-1, keepdims=True), rank)
        before = before + onehot[k]
    rank_ref[...] = rank.astype(jnp.int32)
    run[...] = run[...] + jnp.sum(per_token, axis=0, keepdims=True)
    cnt_ref[...] = run[...].astype(jnp.int32)


def _moe_rank(ti):
    t = ti.shape[0]
    tm = ROW_BLK
    tri = jnp.asarray(np.tril(np.ones((tm, tm), np.float32), -1), BF16)
    blk = pl.BlockSpec((tm, 128), lambda i: (i, 0))
    return pl.pallas_call(
        _rank_kernel,
        out_shape=(jax.ShapeDtypeStruct((t, 128), jnp.int32), jax.ShapeDtypeStruct((1, 128), jnp.int32)),
        grid=(t // tm,),
        in_specs=[blk, pl.BlockSpec((tm, tm), lambda i: (0, 0))],
        out_specs=(blk, pl.BlockSpec((1, 128), lambda i: (0, 0))),
        scratch_shapes=[pltpu.VMEM((1, 128), F32)],
        compiler_params=_params(("arbitrary",)),
        name="moe_rank",
    )(ti, tri)


def _moe_plan(ti):
    t = ti.shape[0]
    n_pairs = t * TOP_K
    tm = MOE_TM
    e_flat = ti[:, :TOP_K].reshape(-1)
    rank, counts = _moe_rank(ti)
    rank = rank[:, :TOP_K].reshape(-1)
    counts = counts[0, :N_EXPERTS]
    padded = (counts + tm - 1) // tm * tm
    pend = jnp.cumsum(padded)
    pstart = pend - padded
    dest = (pstart[e_flat] + rank).astype(jnp.int32)
    n_blocks = (n_pairs + N_EXPERTS * (tm - 1) + tm - 1) // tm
    n_rows = n_blocks * tm
    tok = jnp.arange(n_pairs, dtype=jnp.int32) // TOP_K
    row_tok = jnp.zeros((n_rows,), jnp.int32).at[dest].set(tok, unique_indices=True)
    n_used = (pend[-1] // tm).astype(jnp.int32)
    blk = jnp.arange(n_blocks, dtype=jnp.int32)
    start = jnp.minimum(blk, n_used - 1) * tm
    block_e = jnp.sum((pend[None, :] <= start[:, None]).astype(jnp.int32), axis=1)
    block_e = jnp.minimum(block_e, N_EXPERTS - 1).astype(jnp.int32)
    return dest, row_tok, block_e, n_used.reshape(1), n_blocks


def _moe(h, top_i, layer, w_gu, b_gu, w_down, b_down):
    dest, row_tok, block_e, n_used, n_blocks = _moe_plan(top_i)
    xs = _moe_gather(h, row_tok, n_used, n_blocks)
    yb = _moe_experts(xs, block_e, n_used, layer, w_gu, b_gu, w_down, b_down)
    return yb, dest


def _conv_ctx_kernel(u_ref, w_ref, b_ref, o_ref):
    u = u_ref[...]
    n = u.shape[0]
    row = lax.broadcasted_iota(jnp.int32, u.shape, 0)
    w = w_ref[...]
    um2 = jnp.where(row >= 2, pltpu.roll(u, 2, 0), 0.0)
    um1 = jnp.where(row >= 1, pltpu.roll(u, 1, 0), 0.0)
    up1 = jnp.where(row < n - 1, pltpu.roll(u, n - 1, 0), 0.0)
    o_ref[...] = w[0:1] * um2 + w[1:2] * um1 + w[2:3] * u + w[3:4] * up1 + b_ref[...]


def _conv_ctx(z1, conv_w, conv_b, n_x, n_ctx):
    b, l, _ = z1.shape
    w = conv_w.shape[1]
    return pl.pallas_call(
        _conv_ctx_kernel,
        out_shape=jax.ShapeDtypeStruct((b, n_ctx, w), F32),
        grid=(b,),
        in_specs=[
            pl.BlockSpec((None, n_ctx, w), lambda b_: (b_, n_x // n_ctx, 1)),
            pl.BlockSpec((CONV_W, w), lambda b_: (0, 0)),
            pl.BlockSpec((1, w), lambda b_: (0, 0)),
        ],
        out_specs=pl.BlockSpec((None, n_ctx, w), lambda b_: (b_, 0, 0)),
        compiler_params=_params(("parallel",)),
        name="lru_conv_ctx",
    )(z1, conv_w, conv_b.reshape(1, w))


def _conv_x_kernel(u_ref, prev_ref, next_ref, w_ref, b_ref, o_ref):
    g = pl.program_id(1)
    ng = pl.num_programs(1)
    r = u_ref.shape[0]
    wg = u_ref.shape[1]
    w = w_ref[...]
    c0, c1, c2, c3 = w[0:1], w[1:2], w[2:3], w[3:4]
    bias = b_ref[...]
    sub = lax.broadcasted_iota(jnp.int32, u_ref.shape[1:], 0)

    def from_prev_col(x, halo):
        return jnp.where(sub == 0, halo, pltpu.roll(x, 1, 0))

    def from_next_col(x, halo):
        return jnp.where(sub == wg - 1, halo, pltpu.roll(x, wg - 1, 0))

    has_prev = (g > 0).astype(F32)
    has_next = (g < ng - 1).astype(F32)
    hp = prev_ref.shape[0]
    u_m1 = from_prev_col(u_ref[r - 1], prev_ref[hp - 1, wg - 1:wg, :] * has_prev)
    u_m2 = from_prev_col(u_ref[r - 2], prev_ref[hp - 2, wg - 1:wg, :] * has_prev)
    u_p = from_next_col(u_ref[0], next_ref[0, 0:1, :] * has_next)

    def interior(i, carry):
        o_ref[i] = c0 * u_ref[i - 2] + c1 * u_ref[i - 1] + c2 * u_ref[i] + c3 * u_ref[i + 1] + bias
        return carry

    lax.fori_loop(2, r - 1, interior, 0)
    o_ref[0] = c0 * u_m2 + c1 * u_m1 + c2 * u_ref[0] + c3 * u_ref[1] + bias
    o_ref[1] = c0 * u_m1 + c1 * u_ref[0] + c2 * u_ref[1] + c3 * u_ref[2] + bias
    o_ref[r - 1] = c0 * u_ref[r - 3] + c1 * u_ref[r - 2] + c2 * u_ref[r - 1] + c3 * u_p + bias


def _conv_x(z1, conv_w, conv_b, n_x):
    b, l, f = z1.shape
    w = conv_w.shape[1]
    rows = n_x // GRID_W
    wg = W_GROUP
    ng = GRID_W // wg
    z4 = z1.reshape(b, l // GRID_W, GRID_W, f)
    halo = 8
    last_halo = rows // halo - 1
    return pl.pallas_call(
        _conv_x_kernel,
        out_shape=jax.ShapeDtypeStruct((b, rows, GRID_W, w), F32),
        grid=(b, ng),
        in_specs=[
            pl.BlockSpec((None, rows, wg, w), lambda b_, g: (b_, 0, g, 1)),
            pl.BlockSpec((None, halo, wg, w), lambda b_, g: (b_, last_halo, jnp.maximum(g - 1, 0), 1)),
            pl.BlockSpec((None, halo, wg, w), lambda b_, g: (b_, 0, jnp.minimum(g + 1, ng - 1), 1)),
            pl.BlockSpec((CONV_W, w), lambda b_, g: (0, 0)),
            pl.BlockSpec((1, w), lambda b_, g: (0, 0)),
        ],
        out_specs=pl.BlockSpec((None, rows, wg, w), lambda b_, g: (b_, 0, g, 0)),
        compiler_params=_params(("parallel", "parallel")),
        name="lru_conv_x",
    )(z4, z4, z4, conv_w, conv_b.reshape(1, w))


def _gates_kernel(u_ref, wa_ref, wx_ref, ba_ref, bx_ref, lam_ref, a_ref, b_ref, wabf, wxbf):
    @pl.when(pl.program_id(0) == 0)
    def _():
        wabf[...] = wa_ref[...].astype(BF16)
        wxbf[...] = wx_ref[...].astype(BF16)

    for d in range(2):
        lam = lam_ref[d]
        nl = -lam
        softplus = jnp.maximum(nl, 0.0) + jnp.log(1.0 + jnp.exp(-jnp.abs(nl)))
        for h in range(LRU_HEADS):
            sl = slice(h * LRU_BLOCK, (h + 1) * LRU_BLOCK)
            u = u_ref[:, sl]
            ub = u.astype(BF16)
            r = _sigmoid(_dot(ub, wabf[d, h]) + ba_ref[d][:, sl])
            ig = _sigmoid(_dot(ub, wxbf[d, h]) + bx_ref[d][:, sl])
            log_a = (-LRU_C) * r * softplus[:, sl]
            a = jnp.exp(log_a)
            a_ref[d, :, sl] = a
            b_ref[d, :, sl] = jnp.sqrt(1.0 - a * a) * (ig * u)


def _gates(cu, w_a, w_x, b_a, b_x, lam):
    m, w = cu.shape
    tm = ROW_BLK
    full5 = pl.BlockSpec(w_a.shape, lambda i: (0, 0, 0, 0))
    vec = pl.BlockSpec((2, 1, w), lambda i: (0, 0, 0))
    out = pl.BlockSpec((2, tm, w), lambda i: (0, i, 0))
    return pl.pallas_call(
        _gates_kernel,
        out_shape=(jax.ShapeDtypeStruct((2, m, w), F32), jax.ShapeDtypeStruct((2, m, w), F32)),
        grid=(m // tm,),
        in_specs=[pl.BlockSpec((tm, w), lambda i: (i, 0)), full5, full5, vec, vec, vec],
        out_specs=(out, out),
        scratch_shapes=[pltpu.VMEM(w_a.shape, BF16), pltpu.VMEM(w_a.shape, BF16)],
        compiler_params=_params(("arbitrary",)),
        name="lru_gates",
    )(cu, w_a, w_x, b_a.reshape(2, 1, w), b_x.reshape(2, 1, w), lam.reshape(2, 1, w))


def _lru_scan_kernel(ax_ref, bx_ref, ac_ref, bc_ref, h_ref, carry, acum, *, rev):
    s = pl.program_id(2)
    rows = ax_ref.shape[0]
    wg = ax_ref.shape[1]
    n_ctx = ac_ref.shape[0]
    ft = ax_ref.shape[2]

    @pl.when(s == 0)
    def _():
        def body(t, h):
            tt = n_ctx - 1 - t if rev else t
            return ac_ref[pl.ds(tt, 1), :] * h + bc_ref[pl.ds(tt, 1), :]

        carry[...] = lax.fori_loop(0, n_ctx, body, jnp.zeros((1, ft), F32))

    @pl.when(s > 0)
    def _():
        def seg(i, c):
            a_run, h_run = c
            r = rows - 1 - i if rev else i
            a = ax_ref[r]
            h_run = a * h_run + bx_ref[r]
            a_run = a_run * a
            h_ref[r] = h_run
            acum[r] = a_run
            return a_run, h_run

        a_tot, h_tot = lax.fori_loop(
            0, rows, seg, (jnp.ones((wg, ft), F32), jnp.zeros((wg, ft), F32)))
        h = carry[...]
        sub = lax.broadcasted_iota(jnp.int32, (wg, ft), 0)
        h_in = jnp.zeros((wg, ft), F32)
        for j in (range(wg - 1, -1, -1) if rev else range(wg)):
            h_in = jnp.where(sub == j, h, h_in)
            h = a_tot[j:j + 1] * h + h_tot[j:j + 1]
        carry[...] = h

        def fix(r, c):
            h_ref[r] = h_ref[r] + acum[r] * h_in
            return c

        lax.fori_loop(0, rows, fix, 0)


def _lru_scan(a_x, b_x, a_c, b_c, d, rev):
    _, b, rows, gw, w = a_x.shape
    n_ctx = a_c.shape[2]
    wg = W_GROUP
    ng = gw // wg
    ft = 1024

    def grp(s):
        k = jnp.maximum(s - 1, 0)
        return ng - 1 - k if rev else k

    xblk = pl.BlockSpec((None, None, rows, wg, ft), lambda b_, f, s: (d, b_, 0, grp(s), f))
    cblk = pl.BlockSpec((None, None, n_ctx, ft), lambda b_, f, s: (d, b_, 0, f))
    return pl.pallas_call(
        functools.partial(_lru_scan_kernel, rev=rev),
        out_shape=jax.ShapeDtypeStruct((b, rows, gw, w), F32),
        grid=(b, w // ft, ng + 1),
        in_specs=[xblk, xblk, cblk, cblk],
        out_specs=pl.BlockSpec((None, rows, wg, ft), lambda b_, f, s: (b_, 0, grp(s), f)),
        scratch_shapes=[pltpu.VMEM((1, ft), F32), pltpu.VMEM((rows, wg, ft), F32)],
        compiler_params=_params(("parallel", "parallel", "arbitrary")),
        name="lru_scan_bwd" if rev else "lru_scan_fwd",
    )(a_x, b_x, a_c, b_c)


def _lru_prep_kernel(y_ref, hf_ref, hb_ref, o_ref):
    o_ref[...] = (y_ref[...] * (hf_ref[...] + hb_ref[...])).astype(o_ref.dtype)


def _lru_prep(z1, h_f, h_b, n_x):
    b, l, _ = z1.shape
    w = h_f.shape[-1]
    blk = pl.BlockSpec((None, ROW_BLK, w), lambda b_, t: (b_, t, 0))
    return pl.pallas_call(
        _lru_prep_kernel,
        out_shape=jax.ShapeDtypeStruct((b, n_x, w), BF16),
        grid=(b, n_x // ROW_BLK),
        in_specs=[blk, blk, blk],
        out_specs=blk,
        compiler_params=_params(("parallel", "parallel")),
        name="lru_prep",
    )(z1, h_f, h_b)


def kernel(x, c, ctx, c_ctx, ada_w, ada_b, ln1_g, ln1_b, ln2_g, ln2_b, hg_w_in, hg_lb_logits, hg_norm_g, hg_w_out, lru_w_in, lru_conv_w, lru_conv_b, lru_w_a, lru_b_a, lru_w_x, lru_b_x, lru_lam, lru_w_out, moe_w_router, moe_b_router, moe_w_gu, moe_b_gu, moe_w_down, moe_b_down):
    bsz, n_x, d = x.shape
    n_ctx = ctx.shape[1]
    assert d == D_MODEL and ada_w.shape[0] == DEPTH == 2
    assert n_x % ROW_BLK == 0 and n_ctx == ROW_BLK and n_x % (GRID_W * 8) == 0 and bsz < 8
    l = n_x + n_ctx
    n_x_blocks = n_x // ROW_BLK
    n_blocks = l // ROW_BLK
    ctx_row = bsz

    cv = jnp.zeros((8, d), F32).at[:bsz].set(c).at[bsz].set(c_ctx)
    mod = _ada(cv, ada_w, ada_b).reshape(DEPTH, 8, 6, 1, d)
    xcat = jnp.concatenate([x, ctx], axis=1)

    h0 = _modulate(xcat, mod[0], n_x_blocks, ctx_row)
    lb = jnp.cumsum(jax.nn.softmax(hg_lb_logits.astype(F32), axis=1), axis=1)[:, 0]
    lb_row = jnp.zeros((1, 5 * d), F32).at[0, d:3 * d].set(lb.reshape(-1))
    z = _matmul(h0.reshape(bsz * l, d), hg_w_in[0], mode="hgrn_in", lb_row=lb_row, section=d)
    o2 = _hgrn_scan(z.reshape(bsz, l, 5 * d), n_x)
    a0 = _hgrn_prep(o2.reshape(2, bsz * l, d), z, hg_norm_g[0])
    y0 = _matmul(a0, hg_w_out[0]).reshape(bsz, l, d)
    x1, h1, ti, tg = _post(xcat, y0, mod[0], ln1_g[0], ln1_b[0], moe_w_router[0], moe_b_router[0],
                           n_blocks, n_x_blocks, ctx_row)
    yb, dest = _moe(h1.reshape(bsz * l, d), ti.reshape(bsz * l, 128), 0,
                    moe_w_gu, moe_b_gu, moe_w_down, moe_b_down)
    x2, h2 = _combine(x1, yb, dest, tg, mod[0], ln2_g[0], ln2_b[0], n_x_blocks, ctx_row, mod_next=mod[1])

    z1 = _matmul(h2.reshape(bsz * l, d), lru_w_in[0], mode="lru_in", section=d).reshape(bsz, l, 2 * d)
    cu_x = _conv_x(z1, lru_conv_w[0], lru_conv_b[0], n_x)
    cu_c = _conv_ctx(z1, lru_conv_w[0], lru_conv_b[0], n_x, n_ctx)
    rows = n_x // GRID_W
    a_x, b_x = _gates(cu_x.reshape(bsz * n_x, d), lru_w_a[0], lru_w_x[0], lru_b_a[0], lru_b_x[0], lru_lam[0])
    a_c, b_c = _gates(cu_c.reshape(bsz * n_ctx, d), lru_w_a[0], lru_w_x[0], lru_b_a[0], lru_b_x[0], lru_lam[0])
    a_x = a_x.reshape(2, bsz, rows, GRID_W, d)
    b_x = b_x.reshape(2, bsz, rows, GRID_W, d)
    a_c = a_c.reshape(2, bsz, n_ctx, d)
    b_c = b_c.reshape(2, bsz, n_ctx, d)
    h_f = _lru_scan(a_x, b_x, a_c, b_c, 0, rev=False).reshape(bsz, n_x, d)
    h_b = _lru_scan(a_x, b_x, a_c, b_c, 1, rev=True).reshape(bsz, n_x, d)
    a1 = _lru_prep(z1, h_f, h_b, n_x)
    y1 = _matmul(a1.reshape(bsz * n_x, d), lru_w_out[0]).reshape(bsz, n_x, d)
    x3, h3, ti, tg = _post(x2, y1, mod[1], ln1_g[1], ln1_b[1], moe_w_router[1], moe_b_router[1],
                           n_x_blocks, n_x_blocks, ctx_row)
    yb, dest = _moe(h3.reshape(bsz * n_x, d), ti.reshape(bsz * n_x, 128), 1,
                    moe_w_gu, moe_b_gu, moe_w_down, moe_b_down)
    (out,) = _combine(x3, yb, dest, tg, mod[1], ln2_g[1], ln2_b[1], n_x_blocks, ctx_row)
    return out
```

```python
import functools

import numpy as np
import jax
import jax.numpy as jnp
from jax import lax
from jax.experimental import pallas as pl
from jax.experimental.pallas import tpu as pltpu

F32 = jnp.float32
BF16 = jnp.bfloat16

D_MODEL = 2048
DEPTH = 2
GRID_W = 64
HG_HEADS = 16
HG_DK = 128
LRU_HEADS = 8
LRU_BLOCK = D_MODEL // LRU_HEADS
CONV_W = 4
LRU_C = 8.0
N_EXPERTS = 32
TOP_K = 4
SWIGLU_LIMIT = 7.0
SWIGLU_ALPHA = 1.702
DN_ALPHA = float((2 * DEPTH) ** 0.25)
LN_EPS = 1e-5
RMS_EPS = 1e-6
ROUTER_LANES = 128
ROUTER_PAD = -1e30

ROW_BLK = 256
MM_TM = 1024
MM_TN = 1024
MM_SUB = 256
HG_CHUNK = 128
HG_GROUP = 16
MOE_TM = 512
MOE_TN = 2048
MOE_TN_GU = 2048
CMB_TM = 128
DMA_UNROLL = 8
W_GROUP = 8
VMEM_LIMIT = 56 * 1024 * 1024


def _params(sem):
    return pltpu.CompilerParams(dimension_semantics=sem, vmem_limit_bytes=VMEM_LIMIT)


def _split_bf16(a):
    hi = a.astype(BF16)
    lo = (a - hi.astype(F32)).astype(BF16)
    return hi, lo


def _dot(a, b):
    return jnp.dot(a, b, preferred_element_type=F32)


def _dot3(a, b):
    ah, al = _split_bf16(a)
    bh, bl = _split_bf16(b)
    return _dot(ah, bh) + _dot(ah, bl) + _dot(al, bh)


def _dot_nt(a, b):
    return lax.dot_general(a, b, (((1,), (1,)), ((), ())), preferred_element_type=F32)


def _dot_tn(a, b):
    return lax.dot_general(a, b, (((0,), (0,)), ((), ())), preferred_element_type=F32)


def _sigmoid(x):
    return 0.5 * (1.0 + jnp.tanh(0.5 * x))


def _ada_kernel(cv_ref, w_ref, b_ref, o_ref):
    x = cv_ref[...]
    s = x * _sigmoid(x)
    o_ref[...] = _dot3(s, w_ref[...]) + b_ref[...]


def _ada(cv, ada_w, ada_b):
    depth, d, n = ada_w.shape
    tn = 1024
    return pl.pallas_call(
        _ada_kernel,
        out_shape=jax.ShapeDtypeStruct((depth, 8, n), F32),
        grid=(depth, n // tn),
        in_specs=[
            pl.BlockSpec((8, d), lambda l, j: (0, 0)),
            pl.BlockSpec((None, d, tn), lambda l, j: (l, 0, j)),
            pl.BlockSpec((None, 1, tn), lambda l, j: (l, 0, j)),
        ],
        out_specs=pl.BlockSpec((None, 8, tn), lambda l, j: (l, 0, j)),
        compiler_params=_params(("arbitrary", "arbitrary")),
        name="ada_mod",
    )(cv, ada_w, ada_b.reshape(depth, 1, n))


def _mod_spec(section, n_x_blocks, ctx_row):
    def index(b, t):
        return (jnp.where(t < n_x_blocks, b, ctx_row), section, 0, 0)
    return pl.BlockSpec((None, None, 1, D_MODEL), index)


def _modulate_kernel(x_ref, sh_ref, sc_ref, o_ref):
    o_ref[...] = (x_ref[...] * (1.0 + sc_ref[...]) + sh_ref[...]).astype(o_ref.dtype)


def _modulate(xcat, mod, n_x_blocks, ctx_row):
    b, l, d = xcat.shape
    blk = pl.BlockSpec((None, ROW_BLK, d), lambda b_, t: (b_, t, 0))
    return pl.pallas_call(
        _modulate_kernel,
        out_shape=jax.ShapeDtypeStruct((b, l, d), BF16),
        grid=(b, l // ROW_BLK),
        in_specs=[blk, _mod_spec(0, n_x_blocks, ctx_row), _mod_spec(1, n_x_blocks, ctx_row)],
        out_specs=blk,
        compiler_params=_params(("parallel", "parallel")),
        name="modulate",
    )(xcat, mod, mod)


def _gelu_tanh(x):
    return 0.5 * x * (1.0 + jnp.tanh(0.7978845608028654 * (x + 0.044715 * (x * x * x))))


def _mm_kernel(a_ref, w_ref, *rest, mode, tiles_per_section):
    o_ref, wbf = rest[-2], rest[-1]

    @pl.when(pl.program_id(1) == 0)
    def _():
        wbf[...] = w_ref[...].astype(BF16)

    sec = pl.program_id(0) // tiles_per_section
    tn = o_ref.shape[1]

    def tile(epilogue):
        for c in range(tn // MM_SUB):
            cs = slice(c * MM_SUB, (c + 1) * MM_SUB)
            o_ref[:, cs] = epilogue(_dot(a_ref[...], wbf[:, cs]), cs)

    def identity(acc, cs):
        return acc

    if mode == "plain":
        tile(identity)
    elif mode == "hgrn_in":
        lb_ref = rest[0]

        @pl.when((sec == 0) | (sec == 4))
        def _():
            tile(lambda acc, cs: acc * _sigmoid(acc))

        @pl.when((sec == 1) | (sec == 2))
        def _():
            tile(lambda acc, cs: lb_ref[:, cs] + (1.0 - lb_ref[:, cs]) * _sigmoid(acc))

        @pl.when(sec == 3)
        def _():
            tile(identity)
    elif mode == "lru_in":
        @pl.when(sec == 0)
        def _():
            tile(lambda acc, cs: _gelu_tanh(acc))

        @pl.when(sec == 1)
        def _():
            tile(identity)
    else:
        raise ValueError(mode)


def _matmul(a, w, mode="plain", lb_row=None, section=None):
    m, k = a.shape
    n = w.shape[1]
    tm = MM_TM if m % MM_TM == 0 else ROW_BLK
    tn = MM_TN
    section = section or n
    in_specs = [
        pl.BlockSpec((tm, k), lambda j, i: (i, 0)),
        pl.BlockSpec((k, tn), lambda j, i: (0, j)),
    ]
    args = [a, w]
    if lb_row is not None:
        in_specs.append(pl.BlockSpec((1, tn), lambda j, i: (0, j)))
        args.append(lb_row)
    return pl.pallas_call(
        functools.partial(_mm_kernel, mode=mode, tiles_per_section=section // tn),
        out_shape=jax.ShapeDtypeStruct((m, n), F32),
        grid=(n // tn, m // tm),
        in_specs=in_specs,
        out_specs=pl.BlockSpec((tm, tn), lambda j, i: (i, j)),
        scratch_shapes=[pltpu.VMEM((k, tn), BF16)],
        compiler_params=_params(("arbitrary", "arbitrary")),
        name="matmul_" + mode,
    )(*args)


def _hgrn_tables(c):
    levels = []
    bs = c
    while bs >= 2:
        levels.append(bs)
        bs //= 2
    nlev = len(levels)
    g = np.zeros((2, (2 + nlev) * c, c), np.float32)
    masks = np.zeros((2, nlev + 1, c, c), np.float32)
    r = np.arange(c)
    for d in range(2):
        for row in range(c):
            if d == 0:
                g[d, row, : row + 1] = 1.0
                g[d, c + row, row + 1:] = 1.0
            else:
                g[d, row, row:] = 1.0
                g[d, c + row, :row] = 1.0
        for li, bs in enumerate(levels):
            half = bs // 2
            for row in range(c):
                base = (row // bs) * bs
                mid = base + half
                pos = row - base
                blk = g[d, (2 + li) * c + row]
                if d == 0:
                    if pos >= half:
                        blk[mid: row + 1] = 1.0
                    else:
                        blk[row + 1: mid] = 1.0
                else:
                    if pos < half:
                        blk[row: mid] = 1.0
                    else:
                        blk[mid: row] = 1.0
            same = (r[:, None] // bs) == (r[None, :] // bs)
            t_hi = (r[:, None] % bs) >= half
            s_hi = (r[None, :] % bs) >= half
            if d == 0:
                masks[d, li] = same & t_hi & ~s_hi
            else:
                masks[d, li] = same & ~t_hi & s_hi
        masks[d, nlev] = np.eye(c)
    return jnp.asarray(g, BF16), jnp.asarray(masks, F32), nlev


def _hgrn_scan_kernel(q_ref, f_ref, v_ref, g_ref, m_ref, o_ref, st_ref, *, nlev):
    c = q_ref.shape[0]

    @pl.when(pl.program_id(3) == 0)
    def _():
        st_ref[...] = jnp.zeros_like(st_ref)

    gmat = g_ref[...]
    fwd = pl.program_id(1) == 0
    nbig = nlev - (gmat.shape[0] // c - 1)

    for h in range(HG_GROUP):
        sl = slice(h * HG_DK, (h + 1) * HG_DK)
        f = f_ref[:, sl]
        q = q_ref[:, sl]
        vb = v_ref[:, sl].astype(BF16)
        lf = jnp.log(f)
        k = 1.0 - f
        lf_hi, lf_lo = _split_bf16(lf)
        ex2 = _dot(gmat, jnp.concatenate([lf_hi, lf_lo], axis=1))
        ex = ex2[:, :HG_DK] + ex2[:, HG_DK:]
        cum = ex[0:c]
        total = jnp.where(fwd, cum[c - 1:c], cum[0:1])
        e_lev = []
        bs = c
        for li in range(nbig):
            half = bs // 2
            pieces = []
            for base in range(0, c, bs):
                ref = jnp.where(fwd, cum[base + half - 1:base + half], cum[base + half:base + half + 1])
                pieces.append(-jnp.abs(cum[base:base + bs] - ref))
            e_lev.append(jnp.exp(pieces[0] if len(pieces) == 1 else jnp.concatenate(pieces, axis=0)))
            bs = half
        for li in range(nlev - nbig):
            e_lev.append(jnp.exp(ex[(1 + li) * c:(2 + li) * c]))
        st = st_ref[h]
        o = _dot_nt((q * jnp.exp(cum)).astype(BF16), st.astype(BF16))
        scores = m_ref[nlev] * _dot_nt(q.astype(BF16), k.astype(BF16))
        for li in range(nlev):
            el = e_lev[li]
            scores = scores + m_ref[li] * _dot_nt((q * el).astype(BF16), (k * el).astype(BF16))
        o = o + _dot(scores.astype(BF16), vb)
        o_ref[:, sl] = o
        st_ref[h] = st * jnp.exp(total) + _dot_tn(vb, (k * jnp.exp(total - cum)).astype(BF16))


def _hgrn_scan(z, n_x):
    b, l, _ = z.shape
    c = HG_CHUNK
    nc = l // c
    ncx = n_x // c
    gmat, masks, nlev = _hgrn_tables(c)
    nbig = nlev - 3
    gmat = jnp.concatenate([gmat[:, :c], gmat[:, (2 + nbig) * c:]], axis=1)

    def chunk(d, s):
        fwd = jnp.where(s < nc - ncx, ncx + s, s - (nc - ncx))
        return jnp.where(d == 0, fwd, nc - 1 - s)

    gw = HG_GROUP * HG_DK
    ngrp = D_MODEL // gw

    def spec(section):
        return pl.BlockSpec((None, c, gw), lambda b_, d, g, s: (b_, chunk(d, s), section(d) * ngrp + g))

    return pl.pallas_call(
        functools.partial(_hgrn_scan_kernel, nlev=nlev),
        out_shape=jax.ShapeDtypeStruct((2, b, l, D_MODEL), F32),
        grid=(b, 2, ngrp, nc),
        in_specs=[
            spec(lambda d: 0),
            spec(lambda d: 1 + d),
            spec(lambda d: 3),
            pl.BlockSpec((None,) + gmat.shape[1:], lambda b_, d, g, s: (d, 0, 0)),
            pl.BlockSpec((None,) + masks.shape[1:], lambda b_, d, g, s: (d, 0, 0, 0)),
        ],
        out_specs=pl.BlockSpec((None, None, c, gw), lambda b_, d, g, s: (d, b_, chunk(d, s), g)),
        scratch_shapes=[pltpu.VMEM((HG_GROUP, HG_DK, HG_DK), F32)],
        compiler_params=_params(("parallel", "arbitrary", "arbitrary", "arbitrary")),
        name="hgrn_scan",
    )(z, z, z, gmat, masks)


def _hgrn_prep_kernel(of_ref, ob_ref, g_ref, ng_ref, o_ref):
    ng = ng_ref[...]
    for h in range(HG_HEADS):
        sl = slice(h * HG_DK, (h + 1) * HG_DK)
        o = of_ref[:, sl] + ob_ref[:, sl]
        ms = jnp.mean(o * o, axis=-1, keepdims=True)
        y = o * lax.rsqrt(ms + RMS_EPS) * ng
        o_ref[:, sl] = (y * g_ref[:, sl]).astype(o_ref.dtype)


def _hgrn_prep(o2, z, norm_g):
    _, m, d = o2.shape
    tm = 512 if m % 512 == 0 else ROW_BLK
    return pl.pallas_call(
        _hgrn_prep_kernel,
        out_shape=jax.ShapeDtypeStruct((m, d), BF16),
        grid=(m // tm,),
        in_specs=[
            pl.BlockSpec((None, tm, d), lambda i: (0, i, 0)),
            pl.BlockSpec((None, tm, d), lambda i: (1, i, 0)),
            pl.BlockSpec((tm, d), lambda i: (i, 4)),
            pl.BlockSpec((1, HG_DK), lambda i: (0, 0)),
        ],
        out_specs=pl.BlockSpec((tm, d), lambda i: (i, 0)),
        compiler_params=_params(("parallel",)),
        name="hgrn_prep",
    )(o2, o2, z, norm_g.reshape(1, HG_DK))


def _layer_norm(v, g, b):
    mu = jnp.mean(v, axis=-1, keepdims=True)
    vc = v - mu
    var = jnp.mean(vc * vc, axis=-1, keepdims=True)
    return vc * lax.rsqrt(var + LN_EPS) * g + b


def _router_topk(h, wr, br, ti_ref, tg_ref):
    logits = _dot3(h, wr) + br
    lane = lax.broadcasted_iota(jnp.int32, logits.shape, 1)
    lane_f = lane.astype(F32)
    vals, ids = [], []
    cur = logits
    for _ in range(TOP_K):
        mx = jnp.max(cur, axis=-1, keepdims=True)
        idx = jnp.min(jnp.where(cur == mx, lane_f, float(ROUTER_LANES)), axis=-1, keepdims=True)
        vals.append(mx)
        ids.append(idx)
        cur = jnp.where(lane_f == idx, -jnp.inf, cur)
    ex = [jnp.exp(v - vals[0]) for v in vals]
    den = ex[0] + ex[1] + ex[2] + ex[3]
    ti = jnp.zeros(logits.shape, F32)
    tg = jnp.zeros(logits.shape, F32)
    for kk in range(TOP_K):
        ti = jnp.where(lane == kk, ids[kk], ti)
        tg = jnp.where(lane == kk, ex[kk] / den, tg)
    ti_ref[...] = ti.astype(jnp.int32)
    tg_ref[...] = tg


def _post_kernel(x_ref, y_ref, gate_ref, lng_ref, lnb_ref, sh_ref, sc_ref, wr_ref, br_ref,
                 xo_ref, h_ref, ti_ref, tg_ref):
    v = DN_ALPHA * x_ref[...] + gate_ref[...] * y_ref[...]
    xn = _layer_norm(v, lng_ref[...], lnb_ref[...])
    xo_ref[...] = xn
    h = xn * (1.0 + sc_ref[...]) + sh_ref[...]
    h_ref[...] = h
    _router_topk(h, wr_ref[...], br_ref[...], ti_ref, tg_ref)


def _post(x, y, mod, ln_g, ln_b, w_router, b_router, n_blocks, n_x_blocks, ctx_row):
    b, _, d = x.shape
    rows = n_blocks * ROW_BLK
    blk = pl.BlockSpec((None, ROW_BLK, d), lambda b_, t: (b_, t, 0))
    lane_blk = pl.BlockSpec((None, ROW_BLK, 128), lambda b_, t: (b_, t, 0))
    row = pl.BlockSpec((1, d), lambda b_, t: (0, 0))
    ms = functools.partial(_mod_spec, n_x_blocks=n_x_blocks, ctx_row=ctx_row)
    wr = jnp.zeros((d, ROUTER_LANES), F32).at[:, :N_EXPERTS].set(w_router)
    br = jnp.full((1, ROUTER_LANES), ROUTER_PAD, F32).at[0, :N_EXPERTS].set(b_router)
    return pl.pallas_call(
        _post_kernel,
        out_shape=(
            jax.ShapeDtypeStruct((b, rows, d), F32),
            jax.ShapeDtypeStruct((b, rows, d), F32),
            jax.ShapeDtypeStruct((b, rows, 128), jnp.int32),
            jax.ShapeDtypeStruct((b, rows, 128), F32),
        ),
        grid=(b, n_blocks),
        in_specs=[blk, blk, ms(2), row, row, ms(3), ms(4),
                  pl.BlockSpec((d, ROUTER_LANES), lambda b_, t: (0, 0)),
                  pl.BlockSpec((1, ROUTER_LANES), lambda b_, t: (0, 0))],
        out_specs=(blk, blk, lane_blk, lane_blk),
        compiler_params=_params(("parallel", "parallel")),
        name="post_mixer",
    )(x, y, mod, ln_g.reshape(1, d), ln_b.reshape(1, d), mod, mod, wr, br)


def _gather_kernel(nu_ref, idx_ref, h_hbm, o_ref, buf, sem):
    i = pl.program_id(0)
    tm = o_ref.shape[0]
    nu = nu_ref[0]

    def issue(block, slot):
        base = block * tm

        def start(r8, carry):
            for u in range(DMA_UNROLL):
                r = r8 * DMA_UNROLL + u
                tok = idx_ref[base + r]
                pltpu.make_async_copy(
                    h_hbm.at[pl.ds(tok, 1), :], buf.at[slot, pl.ds(r, 1), :], sem.at[slot]
                ).start(priority=u % 2)
            return carry

        lax.fori_loop(0, tm // DMA_UNROLL, start, 0)

    @pl.when(i == 0)
    def _():
        issue(0, 0)

    @pl.when(i + 1 < nu)
    def _():
        issue(i + 1, (i + 1) % 2)

    @pl.when(i < nu)
    def _():
        slot = i % 2
        pltpu.make_async_copy(buf.at[slot], buf.at[slot], sem.at[slot]).wait()
        o_ref[...] = buf[slot].astype(o_ref.dtype)

    @pl.when(i >= nu)
    def _():
        o_ref[...] = jnp.zeros_like(o_ref)


def _moe_gather(h, row_tok, n_used, n_blocks):
    t, d = h.shape
    tm = MOE_TM
    return pl.pallas_call(
        _gather_kernel,
        out_shape=jax.ShapeDtypeStruct((n_blocks * tm, d), BF16),
        grid_spec=pltpu.PrefetchScalarGridSpec(
            num_scalar_prefetch=2,
            grid=(n_blocks,),
            in_specs=[pl.BlockSpec(memory_space=pl.ANY)],
            out_specs=pl.BlockSpec((tm, d), lambda i, nu, idx: (i, 0)),
            scratch_shapes=[pltpu.VMEM((2, tm, d), F32), pltpu.SemaphoreType.DMA((2,))],
        ),
        compiler_params=_params(("arbitrary",)),
        name="moe_gather",
    )(n_used, row_tok, h)


def _expert_changed(be_ref, i):
    return (i == 0) | (be_ref[i] != be_ref[jnp.maximum(i - 1, 0)])


def _moe_gu_kernel(be_ref, nu_ref, x_ref, w_ref, b_ref, perm_ref, o_ref, wbf):
    i = pl.program_id(1)
    tn = w_ref.shape[1]
    sub = perm_ref.shape[0]
    half = sub // 2

    @pl.when(i < nu_ref[0])
    def _():
        @pl.when(_expert_changed(be_ref, i))
        def _():
            for c in range(tn // sub):
                cs = slice(c * sub, (c + 1) * sub)
                wbf[:, cs] = _dot(w_ref[:, cs].astype(BF16), perm_ref[...]).astype(BF16)

        for c in range(tn // sub):
            cs = slice(c * sub, (c + 1) * sub)
            gu = _dot(x_ref[...], wbf[:, cs]) + b_ref[:, cs]
            g = jnp.minimum(gu[:, :half], SWIGLU_LIMIT)
            u = jnp.clip(gu[:, half:], -SWIGLU_LIMIT, SWIGLU_LIMIT)
            act = (u + 1.0) * (g * _sigmoid(SWIGLU_ALPHA * g))
            o_ref[:, c * half:(c + 1) * half] = act.astype(o_ref.dtype)

    @pl.when(i >= nu_ref[0])
    def _():
        o_ref[...] = jnp.zeros_like(o_ref)


def _moe_down_kernel(be_ref, nu_ref, a_ref, w_ref, b_ref, o_ref, wbf):
    i = pl.program_id(1)

    @pl.when(i < nu_ref[0])
    def _():
        @pl.when(_expert_changed(be_ref, i))
        def _():
            wbf[...] = w_ref[...].astype(BF16)

        o_ref[...] = _dot(a_ref[...], wbf[...]) + b_ref[...]

    @pl.when(i >= nu_ref[0])
    def _():
        o_ref[...] = jnp.zeros_like(o_ref)


def _moe_experts(xs, block_e, n_used, layer, w_gu, b_gu, w_down, b_down):
    n_rows, d = xs.shape
    tm, tn, tng = MOE_TM, MOE_TN, MOE_TN_GU
    n_blocks = n_rows // tm
    de2 = w_gu.shape[-1]
    de = de2 // 2
    sub = MM_SUB
    half = sub // 2
    perm = np.zeros((sub, sub), np.float32)
    perm[2 * np.arange(half), np.arange(half)] = 1.0
    perm[2 * np.arange(half) + 1, half + np.arange(half)] = 1.0
    perm = jnp.asarray(perm, BF16)
    b_perm = b_gu.reshape(DEPTH, N_EXPERTS, de2 // sub, half, 2)
    b_perm = jnp.swapaxes(b_perm, -1, -2).reshape(DEPTH, N_EXPERTS, 1, de2)

    def row(i, nu):
        return jnp.minimum(i, nu[0] - 1)

    act = pl.pallas_call(
        _moe_gu_kernel,
        out_shape=jax.ShapeDtypeStruct((n_rows, de), BF16),
        grid_spec=pltpu.PrefetchScalarGridSpec(
            num_scalar_prefetch=2,
            grid=(de2 // tng, n_blocks),
            in_specs=[
                pl.BlockSpec((tm, d), lambda j, i, be, nu: (row(i, nu), 0)),
                pl.BlockSpec((None, None, d, tng), lambda j, i, be, nu: (layer, be[i], 0, j)),
                pl.BlockSpec((None, None, 1, tng), lambda j, i, be, nu: (layer, be[i], 0, j)),
                pl.BlockSpec((sub, sub), lambda j, i, be, nu: (0, 0)),
            ],
            out_specs=pl.BlockSpec((tm, tng // 2), lambda j, i, be, nu: (i, j)),
            scratch_shapes=[pltpu.VMEM((d, tng), BF16)],
        ),
        compiler_params=_params(("arbitrary", "arbitrary")),
        name="moe_gate_up",
    )(block_e, n_used, xs, w_gu, b_perm, perm)

    yb = pl.pallas_call(
        _moe_down_kernel,
        out_shape=jax.ShapeDtypeStruct((n_rows, d), F32),
        grid_spec=pltpu.PrefetchScalarGridSpec(
            num_scalar_prefetch=2,
            grid=(d // tn, n_blocks),
            in_specs=[
                pl.BlockSpec((tm, de), lambda j, i, be, nu: (row(i, nu), 0)),
                pl.BlockSpec((None, None, de, tn), lambda j, i, be, nu: (layer, be[i], 0, j)),
                pl.BlockSpec((None, None, 1, tn), lambda j, i, be, nu: (layer, be[i], 0, j)),
            ],
            out_specs=pl.BlockSpec((tm, tn), lambda j, i, be, nu: (i, j)),
            scratch_shapes=[pltpu.VMEM((de, tn), BF16)],
        ),
        compiler_params=_params(("arbitrary", "arbitrary")),
        name="moe_down",
    )(block_e, n_used, act, w_down, b_down.reshape(DEPTH, N_EXPERTS, 1, d))
    return yb


def _combine_kernel(dest_ref, tg_ref, x_ref, gate_ref, lng_ref, lnb_ref, *rest, with_next):
    if with_next:
        sh_ref, sc_ref, yb_hbm, xo_ref, h_ref, buf, sem = rest
    else:
        yb_hbm, xo_ref, buf, sem = rest
    tmc = x_ref.shape[0]
    step = pl.program_id(0) * pl.num_programs(1) + pl.program_id(1)
    n_steps = pl.num_programs(0) * pl.num_programs(1)

    def issue(stp, slot):
        base = stp * (tmc * TOP_K)
        per_iter = DMA_UNROLL // TOP_K

        def start(t2, carry):
            for u in range(per_iter):
                t = t2 * per_iter + u
                for kk in range(TOP_K):
                    row = dest_ref[base + t * TOP_K + kk]
                    pltpu.make_async_copy(
                        yb_hbm.at[pl.ds(row, 1), :], buf.at[slot, kk, pl.ds(t, 1), :], sem.at[slot]
                    ).start(priority=kk % 2)
            return carry

        lax.fori_loop(0, tmc // per_iter, start, 0)

    @pl.when(step == 0)
    def _():
        issue(0, 0)

    @pl.when(step + 1 < n_steps)
    def _():
        issue(step + 1, (step + 1) % 2)

    slot = step % 2
    pltpu.make_async_copy(buf.at[slot], buf.at[slot], sem.at[slot]).wait()

    tg = tg_ref[...]
    y = tg[:, 0:1] * buf[slot, 0]
    for kk in range(1, TOP_K):
        y = y + tg[:, kk:kk + 1] * buf[slot, kk]
    v = DN_ALPHA * x_ref[...] + gate_ref[...] * y
    xn = _layer_norm(v, lng_ref[...], lnb_ref[...])
    xo_ref[...] = xn
    if with_next:
        h_ref[...] = (xn * (1.0 + sc_ref[...]) + sh_ref[...]).astype(h_ref.dtype)


def _combine(x, yb, dest, tg, mod, ln_g, ln_b, n_x_blocks, ctx_row, mod_next=None):
    b, l, d = x.shape
    tmc = CMB_TM
    per = ROW_BLK // tmc
    with_next = mod_next is not None

    def mod_spec(section):
        def index(b_, t, dest_):
            return (jnp.where(t // per < n_x_blocks, b_, ctx_row), section, 0, 0)
        return pl.BlockSpec((None, None, 1, d), index)

    blk = pl.BlockSpec((None, tmc, d), lambda b_, t, dest_: (b_, t, 0))
    row = pl.BlockSpec((1, d), lambda b_, t, dest_: (0, 0))
    in_specs = [pl.BlockSpec((None, tmc, 128), lambda b_, t, dest_: (b_, t, 0)), blk, mod_spec(5), row, row]
    args = [tg, x, mod, ln_g.reshape(1, d), ln_b.reshape(1, d)]
    out_shape = [jax.ShapeDtypeStruct((b, l, d), F32)]
    out_specs = [blk]
    if with_next:
        in_specs += [mod_spec(0), mod_spec(1)]
        args += [mod_next, mod_next]
        out_shape.append(jax.ShapeDtypeStruct((b, l, d), BF16))
        out_specs.append(blk)
    in_specs.append(pl.BlockSpec(memory_space=pl.ANY))
    args.append(yb)
    return pl.pallas_call(
        functools.partial(_combine_kernel, with_next=with_next),
        out_shape=tuple(out_shape),
        grid_spec=pltpu.PrefetchScalarGridSpec(
            num_scalar_prefetch=1,
            grid=(b, l // tmc),
            in_specs=in_specs,
            out_specs=tuple(out_specs),
            scratch_shapes=[pltpu.VMEM((2, TOP_K, tmc, d), F32), pltpu.SemaphoreType.DMA((2,))],
        ),
        compiler_params=_params(("arbitrary", "arbitrary")),
        name="moe_combine",
    )(dest, *args)


def _rank_kernel(ti_ref, tri_ref, rank_ref, cnt_ref, run):
    @pl.when(pl.program_id(0) == 0)
    def _():
        run[...] = jnp.zeros_like(run)

    ti = ti_ref[...]
    lane = lax.broadcasted_iota(jnp.int32, ti.shape, 1)
    onehot = [(lane == ti[:, k:k + 1]).astype(F32) for k in range(TOP_K)]
    per_token = onehot[0]
    for k in range(1, TOP_K):
        per_token = per_token + onehot[k]
    before = _dot(tri_ref[...], per_token.astype(BF16)) + run[...]
    rank = jnp.zeros(ti.shape, F32)
    for k in range(TOP_K):
        rank = jnp.where(lane == k, jnp.sum(onehot[k] * before, axis=-1, keepdims=True), rank)
        before = before + onehot[k]
    rank_ref[...] = rank.astype(jnp.int32)
    run[...] = run[...] + jnp.sum(per_token, axis=0, keepdims=True)
    cnt_ref[...] = run[...].astype(jnp.int32)


def _moe_rank(ti):
    t = ti.shape[0]
    tm = ROW_BLK
    tri = jnp.asarray(np.tril(np.ones((tm, tm), np.float32), -1), BF16)
    blk = pl.BlockSpec((tm, 128), lambda i: (i, 0))
    return pl.pallas_call(
        _rank_kernel,
        out_shape=(jax.ShapeDtypeStruct((t, 128), jnp.int32), jax.ShapeDtypeStruct((1, 128), jnp.int32)),
        grid=(t // tm,),
        in_specs=[blk, pl.BlockSpec((tm, tm), lambda i: (0, 0))],
        out_specs=(blk, pl.BlockSpec((1, 128), lambda i: (0, 0))),
        scratch_shapes=[pltpu.VMEM((1, 128), F32)],
        compiler_params=_params(("arbitrary",)),
        name="moe_rank",
    )(ti, tri)


def _moe_plan(ti):
    t = ti.shape[0]
    n_pairs = t * TOP_K
    tm = MOE_TM
    e_flat = ti[:, :TOP_K].reshape(-1)
    rank, counts = _moe_rank(ti)
    rank = rank[:, :TOP_K].reshape(-1)
    counts = counts[0, :N_EXPERTS]
    padded = (counts + tm - 1) // tm * tm
    pend = jnp.cumsum(padded)
    pstart = pend - padded
    dest = (pstart[e_flat] + rank).astype(jnp.int32)
    n_blocks = (n_pairs + N_EXPERTS * (tm - 1) + tm - 1) // tm
    n_rows = n_blocks * tm
    tok = jnp.arange(n_pairs, dtype=jnp.int32) // TOP_K
    row_tok = jnp.zeros((n_rows,), jnp.int32).at[dest].set(tok, unique_indices=True)
    n_used = (pend[-1] // tm).astype(jnp.int32)
    blk = jnp.arange(n_blocks, dtype=jnp.int32)
    start = jnp.minimum(blk, n_used - 1) * tm
    block_e = jnp.sum((pend[None, :] <= start[:, None]).astype(jnp.int32), axis=1)
    block_e = jnp.minimum(block_e, N_EXPERTS - 1).astype(jnp.int32)
    return dest, row_tok, block_e, n_used.reshape(1), n_blocks


def _moe(h, top_i, layer, w_gu, b_gu, w_down, b_down):
    dest, row_tok, block_e, n_used, n_blocks = _moe_plan(top_i)
    xs = _moe_gather(h, row_tok, n_used, n_blocks)
    yb = _moe_experts(xs, block_e, n_used, layer, w_gu, b_gu, w_down, b_down)
    return yb, dest


def _conv_ctx_kernel(u_ref, w_ref, b_ref, o_ref):
    u = u_ref[...]
    n = u.shape[0]
    row = lax.broadcasted_iota(jnp.int32, u.shape, 0)
    w = w_ref[...]
    um2 = jnp.where(row >= 2, pltpu.roll(u, 2, 0), 0.0)
    um1 = jnp.where(row >= 1, pltpu.roll(u, 1, 0), 0.0)
    up1 = jnp.where(row < n - 1, pltpu.roll(u, n - 1, 0), 0.0)
    o_ref[...] = w[0:1] * um2 + w[1:2] * um1 + w[2:3] * u + w[3:4] * up1 + b_ref[...]


def _conv_ctx(z1, conv_w, conv_b, n_x, n_ctx):
    b, l, _ = z1.shape
    w = conv_w.shape[1]
    return pl.pallas_call(
        _conv_ctx_kernel,
        out_shape=jax.ShapeDtypeStruct((b, n_ctx, w), F32),
        grid=(b,),
        in_specs=[
            pl.BlockSpec((None, n_ctx, w), lambda b_: (b_, n_x // n_ctx, 1)),
            pl.BlockSpec((CONV_W, w), lambda b_: (0, 0)),
            pl.BlockSpec((1, w), lambda b_: (0, 0)),
        ],
        out_specs=pl.BlockSpec((None, n_ctx, w), lambda b_: (b_, 0, 0)),
        compiler_params=_params(("parallel",)),
        name="lru_conv_ctx",
    )(z1, conv_w, conv_b.reshape(1, w))


def _conv_x_kernel(u_ref, prev_ref, next_ref, w_ref, b_ref, o_ref):
    g = pl.program_id(1)
    ng = pl.num_programs(1)
    r = u_ref.shape[0]
    wg = u_ref.shape[1]
    w = w_ref[...]
    c0, c1, c2, c3 = w[0:1], w[1:2], w[2:3], w[3:4]
    bias = b_ref[...]
    sub = lax.broadcasted_iota(jnp.int32, u_ref.shape[1:], 0)

    def from_prev_col(x, halo):
        return jnp.where(sub == 0, halo, pltpu.roll(x, 1, 0))

    def from_next_col(x, halo):
        return jnp.where(sub == wg - 1, halo, pltpu.roll(x, wg - 1, 0))

    has_prev = (g > 0).astype(F32)
    has_next = (g < ng - 1).astype(F32)
    hp = prev_ref.shape[0]
    u_m1 = from_prev_col(u_ref[r - 1], prev_ref[hp - 1, wg - 1:wg, :] * has_prev)
    u_m2 = from_prev_col(u_ref[r - 2], prev_ref[hp - 2, wg - 1:wg, :] * has_prev)
    u_p = from_next_col(u_ref[0], next_ref[0, 0:1, :] * has_next)

    def interior(i, carry):
        o_ref[i] = c0 * u_ref[i - 2] + c1 * u_ref[i - 1] + c2 * u_ref[i] + c3 * u_ref[i + 1] + bias
        return carry

    lax.fori_loop(2, r - 1, interior, 0)
    o_ref[0] = c0 * u_m2 + c1 * u_m1 + c2 * u_ref[0] + c3 * u_ref[1] + bias
    o_ref[1] = c0 * u_m1 + c1 * u_ref[0] + c2 * u_ref[1] + c3 * u_ref[2] + bias
    o_ref[r - 1] = c0 * u_ref[r - 3] + c1 * u_ref[r - 2] + c2 * u_ref[r - 1] + c3 * u_p + bias


def _conv_x(z1, conv_w, conv_b, n_x):
    b, l, f = z1.shape
    w = conv_w.shape[1]
    rows = n_x // GRID_W
    wg = W_GROUP
    ng = GRID_W // wg
    z4 = z1.reshape(b, l // GRID_W, GRID_W, f)
    halo = 8
    last_halo = rows // halo - 1
    return pl.pallas_call(
        _conv_x_kernel,
        out_shape=jax.ShapeDtypeStruct((b, rows, GRID_W, w), F32),
        grid=(b, ng),
        in_specs=[
            pl.BlockSpec((None, rows, wg, w), lambda b_, g: (b_, 0, g, 1)),
            pl.BlockSpec((None, halo, wg, w), lambda b_, g: (b_, last_halo, jnp.maximum(g - 1, 0), 1)),
            pl.BlockSpec((None, halo, wg, w), lambda b_, g: (b_, 0, jnp.minimum(g + 1, ng - 1), 1)),
            pl.BlockSpec((CONV_W, w), lambda b_, g: (0, 0)),
            pl.BlockSpec((1, w), lambda b_, g: (0, 0)),
        ],
        out_specs=pl.BlockSpec((None, rows, wg, w), lambda b_, g: (b_, 0, g, 0)),
        compiler_params=_params(("parallel", "parallel")),
        name="lru_conv_x",
    )(z4, z4, z4, conv_w, conv_b.reshape(1, w))


def _gates_kernel(u_ref, wa_ref, wx_ref, ba_ref, bx_ref, lam_ref, a_ref, b_ref, wabf, wxbf):
    @pl.when(pl.program_id(0) == 0)
    def _():
        wabf[...] = wa_ref[...].astype(BF16)
        wxbf[...] = wx_ref[...].astype(BF16)

    for d in range(2):
        lam = lam_ref[d]
        nl = -lam
        softplus = jnp.maximum(nl, 0.0) + jnp.log(1.0 + jnp.exp(-jnp.abs(nl)))
        for h in range(LRU_HEADS):
            sl = slice(h * LRU_BLOCK, (h + 1) * LRU_BLOCK)
            u = u_ref[:, sl]
            ub = u.astype(BF16)
            r = _sigmoid(_dot(ub, wabf[d, h]) + ba_ref[d][:, sl])
            ig = _sigmoid(_dot(ub, wxbf[d, h]) + bx_ref[d][:, sl])
            log_a = (-LRU_C) * r * softplus[:, sl]
            a = jnp.exp(log_a)
            a_ref[d, :, sl] = a
            b_ref[d, :, sl] = jnp.sqrt(1.0 - a * a) * (ig * u)


def _gates(cu, w_a, w_x, b_a, b_x, lam):
    m, w = cu.shape
    tm = ROW_BLK
    full5 = pl.BlockSpec(w_a.shape, lambda i: (0, 0, 0, 0))
    vec = pl.BlockSpec((2, 1, w), lambda i: (0, 0, 0))
    out = pl.BlockSpec((2, tm, w), lambda i: (0, i, 0))
    return pl.pallas_call(
        _gates_kernel,
        out_shape=(jax.ShapeDtypeStruct((2, m, w), F32), jax.ShapeDtypeStruct((2, m, w), F32)),
        grid=(m // tm,),
        in_specs=[pl.BlockSpec((tm, w), lambda i: (i, 0)), full5, full5, vec, vec, vec],
        out_specs=(out, out),
        scratch_shapes=[pltpu.VMEM(w_a.shape, BF16), pltpu.VMEM(w_a.shape, BF16)],
        compiler_params=_params(("arbitrary",)),
        name="lru_gates",
    )(cu, w_a, w_x, b_a.reshape(2, 1, w), b_x.reshape(2, 1, w), lam.reshape(2, 1, w))


def _lru_scan_kernel(ax_ref, bx_ref, ac_ref, bc_ref, h_ref, carry, acum, *, rev):
    s = pl.program_id(2)
    rows = ax_ref.shape[0]
    wg = ax_ref.shape[1]
    n_ctx = ac_ref.shape[0]
    ft = ax_ref.shape[2]

    @pl.when(s == 0)
    def _():
        def body(t, h):
            tt = n_ctx - 1 - t if rev else t
            return ac_ref[pl.ds(tt, 1), :] * h + bc_ref[pl.ds(tt, 1), :]

        carry[...] = lax.fori_loop(0, n_ctx, body, jnp.zeros((1, ft), F32))

    @pl.when(s > 0)
    def _():
        def seg(i, c):
            a_run, h_run = c
            r = rows - 1 - i if rev else i
            a = ax_ref[r]
            h_run = a * h_run + bx_ref[r]
            a_run = a_run * a
            h_ref[r] = h_run
            acum[r] = a_run
            return a_run, h_run

        a_tot, h_tot = lax.fori_loop(
            0, rows, seg, (jnp.ones((wg, ft), F32), jnp.zeros((wg, ft), F32)))
        h = carry[...]
        sub = lax.broadcasted_iota(jnp.int32, (wg, ft), 0)
        h_in = jnp.zeros((wg, ft), F32)
        for j in (range(wg - 1, -1, -1) if rev else range(wg)):
            h_in = jnp.where(sub == j, h, h_in)
            h = a_tot[j:j + 1] * h + h_tot[j:j + 1]
        carry[...] = h

        def fix(r, c):
            h_ref[r] = h_ref[r] + acum[r] * h_in
            return c

        lax.fori_loop(0, rows, fix, 0)


def _lru_scan(a_x, b_x, a_c, b_c, d, rev):
    _, b, rows, gw, w = a_x.shape
    n_ctx = a_c.shape[2]
    wg = W_GROUP
    ng = gw // wg
    ft = 1024

    def grp(s):
        k = jnp.maximum(s - 1, 0)
        return ng - 1 - k if rev else k

    xblk = pl.BlockSpec((None, None, rows, wg, ft), lambda b_, f, s: (d, b_, 0, grp(s), f))
    cblk = pl.BlockSpec((None, None, n_ctx, ft), lambda b_, f, s: (d, b_, 0, f))
    return pl.pallas_call(
        functools.partial(_lru_scan_kernel, rev=rev),
        out_shape=jax.ShapeDtypeStruct((b, rows, gw, w), F32),
        grid=(b, w // ft, ng + 1),
        in_specs=[xblk, xblk, cblk, cblk],
        out_specs=pl.BlockSpec((None, rows, wg, ft), lambda b_, f, s: (b_, 0, grp(s), f)),
        scratch_shapes=[pltpu.VMEM((1, ft), F32), pltpu.VMEM((rows, wg, ft), F32)],
        compiler_params=_params(("parallel", "parallel", "arbitrary")),
        name="lru_scan_bwd" if rev else "lru_scan_fwd",
    )(a_x, b_x, a_c, b_c)


def _lru_prep_kernel(y_ref, hf_ref, hb_ref, o_ref):
    o_ref[...] = (y_ref[...] * (hf_ref[...] + hb_ref[...])).astype(o_ref.dtype)


def _lru_prep(z1, h_f, h_b, n_x):
    b, l, _ = z1.shape
    w = h_f.shape[-1]
    blk = pl.BlockSpec((None, ROW_BLK, w), lambda b_, t: (b_, t, 0))
    return pl.pallas_call(
        _lru_prep_kernel,
        out_shape=jax.ShapeDtypeStruct((b, n_x, w), BF16),
        grid=(b, n_x // ROW_BLK),
        in_specs=[blk, blk, blk],
        out_specs=blk,
        compiler_params=_params(("parallel", "parallel")),
        name="lru_prep",
    )(z1, h_f, h_b)


def kernel(x, c, ctx, c_ctx, ada_w, ada_b, ln1_g, ln1_b, ln2_g, ln2_b, hg_w_in, hg_lb_logits, hg_norm_g, hg_w_out, lru_w_in, lru_conv_w, lru_conv_b, lru_w_a, lru_b_a, lru_w_x, lru_b_x, lru_lam, lru_w_out, moe_w_router, moe_b_router, moe_w_gu, moe_b_gu, moe_w_down, moe_b_down):
    bsz, n_x, d = x.shape
    n_ctx = ctx.shape[1]
    assert d == D_MODEL and ada_w.shape[0] == DEPTH == 2
    assert n_x % ROW_BLK == 0 and n_ctx == ROW_BLK and n_x % (GRID_W * 8) == 0 and bsz < 8
    l = n_x + n_ctx
    n_x_blocks = n_x // ROW_BLK
    n_blocks = l // ROW_BLK
    ctx_row = bsz

    cv = jnp.zeros((8, d), F32).at[:bsz].set(c).at[bsz].set(c_ctx)
    mod = _ada(cv, ada_w, ada_b).reshape(DEPTH, 8, 6, 1, d)
    xcat = jnp.concatenate([x, ctx], axis=1)

    h0 = _modulate(xcat, mod[0], n_x_blocks, ctx_row)
    lb = jnp.cumsum(jax.nn.softmax(hg_lb_logits.astype(F32), axis=1), axis=1)[:, 0]
    lb_row = jnp.zeros((1, 5 * d), F32).at[0, d:3 * d].set(lb.reshape(-1))
    z = _matmul(h0.reshape(bsz * l, d), hg_w_in[0], mode="hgrn_in", lb_row=lb_row, section=d)
    o2 = _hgrn_scan(z.reshape(bsz, l, 5 * d), n_x)
    a0 = _hgrn_prep(o2.reshape(2, bsz * l, d), z, hg_norm_g[0])
    y0 = _matmul(a0, hg_w_out[0]).reshape(bsz, l, d)
    x1, h1, ti, tg = _post(xcat, y0, mod[0], ln1_g[0], ln1_b[0], moe_w_router[0], moe_b_router[0],
                           n_blocks, n_x_blocks, ctx_row)
    yb, dest = _moe(h1.reshape(bsz * l, d), ti.reshape(bsz * l, 128), 0,
                    moe_w_gu, moe_b_gu, moe_w_down, moe_b_down)
    x2, h2 = _combine(x1, yb, dest, tg, mod[0], ln2_g[0], ln2_b[0], n_x_blocks, ctx_row, mod_next=mod[1])

    z1 = _matmul(h2.reshape(bsz * l, d), lru_w_in[0], mode="lru_in", section=d).reshape(bsz, l, 2 * d)
    cu_x = _conv_x(z1, lru_conv_w[0], lru_conv_b[0], n_x)
    cu_c = _conv_ctx(z1, lru_conv_w[0], lru_conv_b[0], n_x, n_ctx)
    rows = n_x // GRID_W
    a_x, b_x = _gates(cu_x.reshape(bsz * n_x, d), lru_w_a[0], lru_w_x[0], lru_b_a[0], lru_b_x[0], lru_lam[0])
    a_c, b_c = _gates(cu_c.reshape(bsz * n_ctx, d), lru_w_a[0], lru_w_x[0], lru_b_a[0], lru_b_x[0], lru_lam[0])
    a_x = a_x.reshape(2, bsz, rows, GRID_W, d)
    b_x = b_x.reshape(2, bsz, rows, GRID_W, d)
    a_c = a_c.reshape(2, bsz, n_ctx, d)
    b_c = b_c.reshape(2, bsz, n_ctx, d)
    h_f = _lru_scan(a_x, b_x, a_c, b_c, 0, rev=False).reshape(bsz, n_x, d)
    h_b = _lru_scan(a_x, b_x, a_c, b_c, 1, rev=True).reshape(bsz, n_x, d)
    a1 = _lru_prep(z1, h_f, h_b, n_x)
    y1 = _matmul(a1.reshape(bsz * n_x, d), lru_w_out[0]).reshape(bsz, n_x, d)
    x3, h3, ti, tg = _post(x2, y1, mod[1], ln1_g[1], ln1_b[1], moe_w_router[1], moe_b_router[1],
                           n_x_blocks, n_x_blocks, ctx_row)
    yb, dest = _moe(h3.reshape(bsz * n_x, d), ti.reshape(bsz * n_x, 128), 1,
                    moe_w_gu, moe_b_gu, moe_w_down, moe_b_down)
    (out,) = _combine(x3, yb, dest, tg, mod[1], ln2_g[1], ln2_b[1], n_x_blocks, ctx_row)
    return out
```
